```python
import jax, jax.numpy as jnp
from jax import lax
import numpy as np

D_MODEL = 1024
BATCH = 16
SEQ = 4096
DEPTH = 1

HEAD_DIM = 64
D_MIX = D_MODEL
RWKV_WIDTH = D_MIX // 2
FOX_WIDTH = D_MIX - RWKV_WIDTH
RWKV_HEADS = RWKV_WIDTH // HEAD_DIM
FOX_HEADS = FOX_WIDTH // HEAD_DIM
DECAY_LORA = 64
AAA_LORA = 64
GATE_LORA = 128
RWKV_COLS = 3 * RWKV_WIDTH + DECAY_LORA + AAA_LORA + GATE_LORA
FOX_COLS = 3 * FOX_WIDTH + FOX_HEADS
IN_COLS = RWKV_COLS + FOX_COLS
Q_BLOCK = 128
N_GROUPS = 4
EXPERTS_PER_GROUP = 8
N_EXPERTS = N_GROUPS * EXPERTS_PER_GROUP
TOP_K = 2
D_EXPERT = D_MODEL // 2
MOE_BLOCK = 128
NORM_EPS = 1e-6
GN_EPS = 64e-5

kernel_name = "hymba_rwkv7_fox_hier_moe_adaln"


def rms_norm(x, g):
    xf = x.astype(jnp.float32)
    y = xf * lax.rsqrt(jnp.mean(xf * xf, axis=-1, keepdims=True) + NORM_EPS)
    return (y * g.astype(jnp.float32)).astype(x.dtype)


def wkv7_scan(r, decay, k, v, kk, a):
    B, S, H, N = r.shape

    def step(state, inp):
        r_t, w_t, k_t, v_t, kk_t, a_t = inp
        sa = jnp.einsum("bhvk,bhk->bhv", state, -kk_t)
        state = (state * w_t[:, :, None, :]
                 + sa[..., None] * (kk_t * a_t)[:, :, None, :]
                 + v_t[..., :, None] * k_t[:, :, None, :])
        y = jnp.einsum("bhvk,bhk->bhv", state, r_t)
        return state, y

    xs = tuple(jnp.moveaxis(t, 1, 0) for t in (r, decay, k, v, kk, a))
    init = jnp.zeros((B, H, N, N), jnp.float32)
    _, ys = lax.scan(step, init, xs)
    return jnp.moveaxis(ys, 0, 1)


def rwkv7_time_mix(p, mu, w0, w_up, a0, a_up, g_up, k_k, k_a, r_k, lnx_g, lnx_b):
    B, S, _ = p.shape
    H, N, C = RWKV_HEADS, HEAD_DIM, RWKV_WIDTH
    f32 = jnp.float32
    p_prev = jnp.pad(p, ((0, 0), (1, 0), (0, 0)))[:, :-1]
    p = p + (p_prev - p) * mu
    r, k, v, wd, ad, gd = jnp.split(
        p, [C, 2 * C, 3 * C, 3 * C + DECAY_LORA, 3 * C + DECAY_LORA + AAA_LORA], axis=-1)
    w_log = -jax.nn.softplus(-(w0 + jnp.tanh(wd) @ w_up).astype(f32)) - 0.5
    decay = jnp.exp(-jnp.exp(w_log))
    a = jax.nn.sigmoid((a0 + ad @ a_up).astype(f32))
    g = (jax.nn.sigmoid(gd) @ g_up).astype(f32)
    heads = lambda t: t.astype(f32).reshape(B, S, H, N)
    r, k, v, decay, a = heads(r), heads(k), heads(v), heads(decay), heads(a)
    kk = k * k_k.astype(f32).reshape(H, N)
    kk = kk / jnp.maximum(jnp.sqrt(jnp.sum(kk * kk, axis=-1, keepdims=True)), 1e-12)
    k = k * (1.0 + (a - 1.0) * k_a.astype(f32).reshape(H, N))
    y = wkv7_scan(r, decay, k, v, kk, a)
    mean = jnp.mean(y, axis=-1, keepdims=True)
    var = jnp.mean(jnp.square(y - mean), axis=-1, keepdims=True)
    y = ((y - mean) * lax.rsqrt(var + GN_EPS)).reshape(B, S, C) * lnx_g + lnx_b
    bonus = jnp.sum(r * k * r_k.astype(f32), axis=-1, keepdims=True) * v
    y = y + bonus.reshape(B, S, C)
    return (y * g).astype(p.dtype)


def forgetting_attention(p, f_bias):
    B, S, _ = p.shape
    H, N, C = FOX_HEADS, HEAD_DIM, FOX_WIDTH
    f32 = jnp.float32
    q, k, v, f = jnp.split(p, [C, 2 * C, 3 * C], axis=-1)
    to_heads = lambda t: t.reshape(B, S, H, N).transpose(0, 2, 1, 3)
    q = to_heads(q).astype(f32) * (N ** -0.5)
    k = to_heads(k).astype(f32)
    v = to_heads(v)
    log_f = jax.nn.log_sigmoid((f + f_bias).astype(f32))
    cum = jnp.cumsum(log_f, axis=1).transpose(0, 2, 1)
    outs = []
    for i in range(S // Q_BLOCK):
        q0, q1 = i * Q_BLOCK, (i + 1) * Q_BLOCK
        s = jnp.einsum("bhqd,bhkd->bhqk", q[:, :, q0:q1], k[:, :, :q1])
        s = s + cum[:, :, q0:q1, None] - cum[:, :, None, :q1]
        causal = jnp.arange(q0, q1)[:, None] >= jnp.arange(q1)[None, :]
        s = jnp.where(causal, s, -jnp.inf)
        prob = jax.nn.softmax(s, axis=-1)
        outs.append(jnp.einsum("bhqk,bhkd->bhqd", prob.astype(v.dtype), v[:, :, :q1]))
    o = jnp.concatenate(outs, axis=2)
    return o.transpose(0, 2, 1, 3).reshape(B, S, C)


def hier_moe(h, w_grp, b_grp, w_rt, b_rt, w_gate, w_up, w_down):
    T, D = h.shape
    f32 = jnp.float32
    grp_prob = jax.nn.softmax((h @ w_grp).astype(f32) + b_grp, axis=-1)
    g_sel = jnp.argmax(grp_prob, axis=-1)
    p_g = jnp.take_along_axis(grp_prob, g_sel[:, None], axis=-1)
    e_logits = ((h @ w_rt).astype(f32) + b_rt).reshape(T, N_GROUPS, EXPERTS_PER_GROUP)
    e_logits = jnp.take_along_axis(e_logits, g_sel[:, None, None], axis=1)[:, 0]
    top_p, top_i = lax.top_k(jax.nn.softmax(e_logits, axis=-1), TOP_K)
    weights = p_g * top_p / jnp.sum(top_p, axis=-1, keepdims=True)
    expert_id = g_sel[:, None].astype(jnp.int32) * EXPERTS_PER_GROUP + top_i.astype(jnp.int32)

    A = T * TOP_K
    flat_e = expert_id.reshape(A)
    flat_w = weights.reshape(A)
    flat_tok = jnp.repeat(jnp.arange(T, dtype=jnp.int32), TOP_K)
    order = jnp.argsort(flat_e)
    se = flat_e[order]
    counts = jnp.bincount(flat_e, length=N_EXPERTS)
    starts = jnp.cumsum(counts) - counts
    padded = (counts + MOE_BLOCK - 1) // MOE_BLOCK * MOE_BLOCK
    pends = jnp.cumsum(padded)
    pstarts = pends - padded
    dest = pstarts[se] + jnp.arange(A, dtype=jnp.int32) - starts[se]
    P = (A + N_EXPERTS * (MOE_BLOCK - 1) + MOE_BLOCK - 1) // MOE_BLOCK * MOE_BLOCK
    n_blocks = P // MOE_BLOCK
    buf_tok = jnp.zeros((P,), jnp.int32).at[dest].set(flat_tok[order])
    buf_w = jnp.zeros((P,), f32).at[dest].set(flat_w[order])
    blk_e = jnp.searchsorted(pends, jnp.arange(n_blocks, dtype=pends.dtype) * MOE_BLOCK, side="right")
    blk_e = jnp.minimum(blk_e, N_EXPERTS - 1).astype(jnp.int32)

    def expert_block(args):
        tok, wgt, e = args
        xb = h[tok]
        hid = jax.nn.silu(xb @ w_gate[e]) * (xb @ w_up[e])
        y = hid @ w_down[e]
        return y * wgt[:, None].astype(y.dtype)

    ys = lax.map(expert_block, (buf_tok.reshape(n_blocks, MOE_BLOCK),
                                buf_w.reshape(n_blocks, MOE_BLOCK), blk_e))
    return jax.ops.segment_sum(ys.reshape(P, D), buf_tok, num_segments=T)


def setup_inputs(seed: int = 0) -> dict:
    key = jax.random.key(seed)
    ks = jax.random.split(key, 28)
    nrm = lambda i, shape, scale: scale * jax.random.normal(ks[i], shape, jnp.float32)
    L, D, C = DEPTH, D_MODEL, RWKV_WIDTH
    chan = jnp.arange(C, dtype=jnp.float32) / (C - 1)
    w0_base = -7.0 + 5.0 * chan ** 0.85 + 0.5
    return {
        "x": nrm(0, (BATCH, SEQ, D), 1.0),
        "c": nrm(1, (BATCH, D), 1.0),
        "w_ada": nrm(2, (L, D, 6 * D), 0.5 * D ** -0.5),
        "b_ada": nrm(3, (L, 6 * D), 0.02),
        "norm1_g": 1.0 + nrm(4, (L, D), 0.05),
        "w_in": nrm(5, (L, D, IN_COLS), D ** -0.5),
        "rwkv_mu": jax.random.uniform(ks[6], (L, RWKV_COLS), jnp.float32),
        "rwkv_w0": w0_base + nrm(7, (L, C), 0.1),
        "rwkv_w_up": nrm(8, (L, DECAY_LORA, C), 0.5 * DECAY_LORA ** -0.5),
        "rwkv_a0": nrm(9, (L, C), 0.1),
        "rwkv_a_up": nrm(10, (L, AAA_LORA, C), AAA_LORA ** -0.5),
        "rwkv_g_up": nrm(11, (L, GATE_LORA, C), GATE_LORA ** -0.5),
        "rwkv_k_k": 0.85 + nrm(12, (L, C), 0.05),
        "rwkv_k_a": 1.0 + nrm(13, (L, C), 0.05),
        "rwkv_r_k": nrm(14, (L, RWKV_HEADS, HEAD_DIM), 0.1),
        "rwkv_lnx_g": 1.0 + nrm(15, (L, C), 0.05),
        "rwkv_lnx_b": nrm(16, (L, C), 0.02),
        "fox_f_bias": jnp.linspace(1.0, 5.0, FOX_HEADS, dtype=jnp.float32) + nrm(17, (L, FOX_HEADS), 0.1),
        "w_out": nrm(18, (L, D_MIX, D), D_MIX ** -0.5),
        "norm2_g": 1.0 + nrm(19, (L, D), 0.05),
        "moe_w_grp": nrm(20, (L, D, N_GROUPS), D ** -0.5),
        "moe_b_grp": nrm(21, (L, N_GROUPS), 0.01),
        "moe_w_rt": nrm(22, (L, D, N_EXPERTS), D ** -0.5),
        "moe_b_rt": nrm(23, (L, N_EXPERTS), 0.01),
        "moe_w_gate": nrm(24, (L, N_EXPERTS, D, D_EXPERT), D ** -0.5),
        "moe_w_up": nrm(25, (L, N_EXPERTS, D, D_EXPERT), D ** -0.5),
        "moe_w_down": nrm(26, (L, N_EXPERTS, D_EXPERT, D), D_EXPERT ** -0.5),
        "norm_f_g": 1.0 + nrm(27, (D,), 0.05),
    }


def reference(x, c, w_ada, b_ada, norm1_g, w_in, rwkv_mu, rwkv_w0, rwkv_w_up, rwkv_a0, rwkv_a_up,
              rwkv_g_up, rwkv_k_k, rwkv_k_a, rwkv_r_k, rwkv_lnx_g, rwkv_lnx_b, fox_f_bias, w_out,
              norm2_g, moe_w_grp, moe_b_grp, moe_w_rt, moe_b_rt, moe_w_gate, moe_w_up, moe_w_down,
              norm_f_g):
    B, S, D = x.shape
    cond = jax.nn.silu(c)
    for l in range(DEPTH):
        mod = (cond @ w_ada[l] + b_ada[l])[:, None, :]
        sh1, sc1, gt1, sh2, sc2, gt2 = jnp.split(mod, 6, axis=-1)
        h = rms_norm(x, norm1_g[l]) * (1.0 + sc1) + sh1
        proj = h @ w_in[l]
        y_rwkv = rwkv7_time_mix(proj[..., :RWKV_COLS], rwkv_mu[l], rwkv_w0[l], rwkv_w_up[l],
                                rwkv_a0[l], rwkv_a_up[l], rwkv_g_up[l], rwkv_k_k[l], rwkv_k_a[l],
                                rwkv_r_k[l], rwkv_lnx_g[l], rwkv_lnx_b[l])
        y_fox = forgetting_attention(proj[..., RWKV_COLS:], fox_f_bias[l])
        mix = jnp.concatenate([y_rwkv, y_fox], axis=-1) @ w_out[l]
        x = x + gt1 * mix
        h = rms_norm(x, norm2_g[l]) * (1.0 + sc2) + sh2
        ff = hier_moe(h.reshape(B * S, D), moe_w_grp[l], moe_b_grp[l], moe_w_rt[l], moe_b_rt[l],
                      moe_w_gate[l], moe_w_up[l], moe_w_down[l]).reshape(B, S, D)
        x = x + gt2 * ff
    return rms_norm(x, norm_f_g)
```

```python
import functools

import jax
import jax.numpy as jnp
from jax import lax
from jax.experimental import pallas as pl
from jax.experimental.pallas import tpu as pltpu

F32 = jnp.float32
BF16 = jnp.bfloat16
I32 = jnp.int32
HIGHEST = lax.Precision.HIGHEST

HEAD_DIM = 64
N_GROUPS = 4
EXPERTS_PER_GROUP = 8
N_EXPERTS = N_GROUPS * EXPERTS_PER_GROUP
NORM_EPS = 1e-6
GN_EPS = 64e-5
LANES = 128
VMEM_LIMIT = 56 * 1024 * 1024

EXPERT_ROWS = 256
MASK_VALUE = -1e30


def _cparams(sem):
    return pltpu.CompilerParams(dimension_semantics=sem, vmem_limit_bytes=VMEM_LIMIT)


def _sigmoid(x):
    return 1.0 / (1.0 + jnp.exp(-x))


def _softplus(x):
    return jnp.maximum(x, 0.0) + jnp.log(1.0 + jnp.exp(-jnp.abs(x)))


def _rms(x):
    return x * lax.rsqrt(jnp.mean(x * x, axis=-1, keepdims=True) + NORM_EPS)


def _mod_kernel(c_ref, w_ref, b_ref, o_ref):
    c = c_ref[...]
    cond = c * _sigmoid(c)
    o_ref[...] = jnp.dot(cond, w_ref[...], precision=HIGHEST, preferred_element_type=F32) + b_ref[...]


def _adaln_mod(c, w_ada, b_ada):
    B, D = c.shape
    N = w_ada.shape[1]
    tn = D
    return pl.pallas_call(
        _mod_kernel,
        grid=(N // tn,),
        in_specs=[pl.BlockSpec((B, D), lambda j: (0, 0)),
                  pl.BlockSpec((D, tn), lambda j: (0, j)),
                  pl.BlockSpec((1, tn), lambda j: (0, j))],
        out_specs=pl.BlockSpec((B, tn), lambda j: (0, j)),
        out_shape=jax.ShapeDtypeStruct((B, N), F32),
        compiler_params=_cparams(("arbitrary",)),
        name="adaln_mod",
    )(c, w_ada, b_ada.reshape(1, N))


def _proj_kernel(x_ref, mod_ref, g_ref, w_ref, pr_ref, q_ref, k_ref, v_ref, f_ref, *, n_rwkv, c_fox):
    m = mod_ref[0]
    h = (_rms(x_ref[...]) * g_ref[...] * (1.0 + m[1:2]) + m[0:1]).astype(BF16)
    o = n_rwkv
    pr_ref[...] = jnp.dot(h, w_ref[:, 0:o], preferred_element_type=F32)
    q = jnp.dot(h, w_ref[:, o:o + c_fox], preferred_element_type=F32)
    q_ref[...] = (q * (HEAD_DIM ** -0.5)).astype(BF16)
    k_ref[...] = jnp.dot(h, w_ref[:, o + c_fox:o + 2 * c_fox], preferred_element_type=F32).astype(BF16)
    v_ref[...] = jnp.dot(h, w_ref[:, o + 2 * c_fox:o + 3 * c_fox], preferred_element_type=F32).astype(BF16)
    f_ref[...] = jnp.dot(h, w_ref[:, o + 3 * c_fox:o + 3 * c_fox + LANES], preferred_element_type=F32)


def _project(x2, mod3, norm_g, w_pad, S, n_rwkv, c_fox, tm):
    T, D = x2.shape
    NW = w_pad.shape[1]
    row = lambda i: (i, 0)
    return pl.pallas_call(
        functools.partial(_proj_kernel, n_rwkv=n_rwkv, c_fox=c_fox),
        grid=(T // tm,),
        in_specs=[pl.BlockSpec((tm, D), row),
                  pl.BlockSpec((1, 6, D), lambda i: ((i * tm) // S, 0, 0)),
                  pl.BlockSpec((1, D), lambda i: (0, 0)),
                  pl.BlockSpec((D, NW), lambda i: (0, 0))],
        out_specs=[pl.BlockSpec((tm, n_rwkv), row), pl.BlockSpec((tm, c_fox), row),
                   pl.BlockSpec((tm, c_fox), row), pl.BlockSpec((tm, c_fox), row),
                   pl.BlockSpec((tm, LANES), row)],
        out_shape=[jax.ShapeDtypeStruct((T, n_rwkv), F32), jax.ShapeDtypeStruct((T, c_fox), BF16),
                   jax.ShapeDtypeStruct((T, c_fox), BF16), jax.ShapeDtypeStruct((T, c_fox), BF16),
                   jax.ShapeDtypeStruct((T, LANES), F32)],
        compiler_params=_cparams(("arbitrary",)),
        name="in_proj",
    )(x2, mod3, norm_g, w_pad)


def _rwkv_prep_kernel(p_ref, mu_ref, w0_ref, wup_ref, a0_ref, aup_ref, gup_ref,
                      r_ref, k_ref, v_ref, w_ref, a_ref, g_ref, carry, *, c):
    @pl.when(pl.program_id(1) == 0)
    def _():
        carry[...] = jnp.zeros_like(carry)

    p = p_ref[...]
    tt = p.shape[0]
    row = lax.broadcasted_iota(I32, (tt, 1), 0)
    prev = jnp.where(row == 0, carry[...], pltpu.roll(p, 1, 0))
    carry[...] = p[tt - 1:tt, :]
    xs = p + (prev - p) * mu_ref[...]
    r_ref[...] = xs[:, 0:c]
    k_ref[...] = xs[:, c:2 * c]
    v_ref[...] = xs[:, 2 * c:3 * c]
    lo = xs[:, 3 * c:3 * c + LANES]
    wl = jnp.dot(jnp.tanh(lo), wup_ref[...], precision=HIGHEST, preferred_element_type=F32)
    w_log = -_softplus(-(w0_ref[...] + wl)) - 0.5
    w_ref[...] = jnp.exp(-jnp.exp(w_log))
    al = jnp.dot(lo, aup_ref[...], precision=HIGHEST, preferred_element_type=F32)
    a_ref[...] = _sigmoid(a0_ref[...] + al)
    gd = _sigmoid(xs[:, 3 * c + LANES:3 * c + 2 * LANES])
    g_ref[...] = jnp.dot(gd, gup_ref[...], precision=HIGHEST, preferred_element_type=F32)


def _rwkv_prep(p_rwkv, mu, w0, wup_pad, a0, aup_pad, gup, B, S, c, tt):
    T, NR = p_rwkv.shape
    nt = S // tt
    row = lambda b, i: (b * nt + i, 0)
    const = lambda b, i: (0, 0)
    out = jax.ShapeDtypeStruct((T, c), F32)
    return pl.pallas_call(
        functools.partial(_rwkv_prep_kernel, c=c),
        grid=(B, nt),
        in_specs=[pl.BlockSpec((tt, NR), row), pl.BlockSpec((1, NR), const),
                  pl.BlockSpec((1, c), const), pl.BlockSpec((LANES, c), const),
                  pl.BlockSpec((1, c), const), pl.BlockSpec((LANES, c), const),
                  pl.BlockSpec((LANES, c), const)],
        out_specs=[pl.BlockSpec((tt, c), row)] * 6,
        out_shape=[out] * 6,
        scratch_shapes=[pltpu.VMEM((1, NR), F32)],
        compiler_params=_cparams(("arbitrary", "arbitrary")),
        name="rwkv_prep",
    )(p_rwkv, mu, w0, wup_pad, a0, aup_pad, gup)


ROW_UNROLL = 4


def _scan_kernel(r_ref, k_ref, v_ref, w_ref, a_ref, kkc_ref, kac_ref, rkc_ref, lg_ref, lb_ref,
                 y_ref, st, kk_s, b_s, km_s, yraw):
    @pl.when(pl.program_id(0) == 0)
    def _():
        st[...] = jnp.zeros_like(st)

    tc, n, _ = r_ref.shape
    k = k_ref[...]
    a = a_ref[...]
    kkr = k * kkc_ref[...][None]
    nrm = jnp.sqrt(jnp.sum(kkr * kkr, axis=1, keepdims=True))
    kk = kkr / jnp.maximum(nrm, 1e-12)
    kk_s[...] = kk
    b_s[...] = kk * a
    km_s[...] = k * (1.0 + (a - 1.0) * kac_ref[...][None])

    def step(t, carry):
        kk_t = kk_s[t]
        w_t = w_ref[t]
        b_t = b_s[t]
        k_t = km_s[t]
        r_t = r_ref[t]

        def rows(j, c2):
            for u in range(ROW_UNROLL):
                vv = j * ROW_UNROLL + u
                s_old = st[vv]
                sa = -jnp.sum(s_old * kk_t, axis=0, keepdims=True)
                s_new = s_old * w_t + sa * b_t + v_ref[t, pl.ds(vv, 1), :] * k_t
                st[vv] = s_new
                yraw[t, pl.ds(vv, 1), :] = jnp.sum(s_new * r_t, axis=0, keepdims=True)
            return c2

        lax.fori_loop(0, n // ROW_UNROLL, rows, 0)
        return carry

    lax.fori_loop(0, tc, step, 0)

    y = yraw[...]
    mean = jnp.mean(y, axis=1, keepdims=True)
    yc = y - mean
    var = jnp.mean(yc * yc, axis=1, keepdims=True)
    yn = yc * lax.rsqrt(var + GN_EPS) * lg_ref[...][None] + lb_ref[...][None]
    bonus = jnp.sum(r_ref[...] * km_s[...] * rkc_ref[...][None], axis=1, keepdims=True) * v_ref[...]
    y_ref[...] = yn + bonus


def _wkv_scan(r, k, v, w, a, kkc, kac, rkc, lg, lb, tc):
    S, n, L = r.shape
    blk = pl.BlockSpec((tc, n, L), lambda i: (i, 0, 0))
    cst = pl.BlockSpec((n, L), lambda i: (0, 0))
    return pl.pallas_call(
        _scan_kernel,
        grid=(S // tc,),
        in_specs=[blk] * 5 + [cst] * 5,
        out_specs=blk,
        out_shape=jax.ShapeDtypeStruct((S, n, L), F32),
        scratch_shapes=[pltpu.VMEM((n, n, L), F32)] + [pltpu.VMEM((tc, n, L), F32)] * 4,
        compiler_params=_cparams(("arbitrary",)),
        name="wkv_scan",
    )(r, k, v, w, a, kkc, kac, rkc, lg, lb)


def _fcum_kernel(f_ref, fb_ref, tri_ref, cum_ref, cumt_ref, carry, *, nh):
    @pl.when(pl.program_id(1) == 0)
    def _():
        carry[...] = jnp.zeros_like(carry)

    z = f_ref[0] + fb_ref[...]
    lf = -_softplus(-z)
    inc = jnp.dot(tri_ref[...], lf, precision=HIGHEST, preferred_element_type=F32) + carry[...]
    ts = inc.shape[0]
    cum_ref[0] = inc
    cumt_ref[0] = inc.T[0:nh]
    carry[...] = inc[ts - 1:ts, :]


def _forget_cumsum(f3, fb_pad, nh, ts):
    B, S, _ = f3.shape
    tri = (lax.broadcasted_iota(I32, (ts, ts), 0) >= lax.broadcasted_iota(I32, (ts, ts), 1)).astype(F32)
    return pl.pallas_call(
        functools.partial(_fcum_kernel, nh=nh),
        grid=(B, S // ts),
        in_specs=[pl.BlockSpec((1, ts, LANES), lambda b, i: (b, i, 0)),
                  pl.BlockSpec((1, LANES), lambda b, i: (0, 0)),
                  pl.BlockSpec((ts, ts), lambda b, i: (0, 0))],
        out_specs=[pl.BlockSpec((1, ts, LANES), lambda b, i: (b, i, 0)),
                   pl.BlockSpec((1, nh, ts), lambda b, i: (b, 0, i))],
        out_shape=[jax.ShapeDtypeStruct((B, S, LANES), F32), jax.ShapeDtypeStruct((B, nh, S), F32)],
        scratch_shapes=[pltpu.VMEM((1, LANES), F32)],
        compiler_params=_cparams(("arbitrary", "arbitrary")),
        name="forget_cumsum",
    )(f3, fb_pad, tri)


def _fox_kernel(q_ref, k_ref, v_ref, cq_ref, ck_ref, o_ref, *, tq, tk):
    pair = pl.program_id(1)
    qi = pl.program_id(2)
    q = q_ref[0]
    cq_all = cq_ref[0]
    lane = lax.broadcasted_iota(I32, (1, LANES), 1)
    rowg = qi * tq + lax.broadcasted_iota(I32, (tq, tk), 0)
    col0 = lax.broadcasted_iota(I32, (tq, tk), 1)
    nkb = ((qi + 1) * tq) // tk
    out = jnp.zeros((tq, LANES), F32)
    for hh in range(2):
        head = 2 * pair + hh
        hmask = (lane // HEAD_DIM) == hh
        qh = jnp.where(hmask, q, jnp.zeros_like(q))
        cq = jnp.sum(jnp.where(lane == head, cq_all, 0.0), axis=-1, keepdims=True)

        def body(kb, carry, qh=qh, cq=cq, head=head, hmask=hmask):
            m, l, acc = carry
            off = pl.multiple_of(kb * tk, tk)
            ks = k_ref[0, pl.ds(off, tk), :]
            vs = v_ref[0, pl.ds(off, tk), :]
            vs = jnp.where(hmask, vs, jnp.zeros_like(vs))
            ck = ck_ref[0, pl.ds(head, 1), pl.ds(off, tk)]
            s = lax.dot_general(qh, ks, (((1,), (1,)), ((), ())), preferred_element_type=F32)
            s = s + (cq - ck)
            s = jnp.where(rowg >= col0 + off, s, MASK_VALUE)
            m_new = jnp.maximum(m, jnp.max(s, axis=-1, keepdims=True))
            alpha = jnp.exp(m - m_new)
            p = jnp.exp(s - m_new)
            l = alpha * l + jnp.sum(p, axis=-1, keepdims=True)
            acc = alpha * acc + jnp.dot(p.astype(BF16), vs, preferred_element_type=F32)
            return m_new, l, acc

        init = (jnp.full((tq, 1), MASK_VALUE, F32), jnp.zeros((tq, 1), F32), jnp.zeros((tq, LANES), F32))
        _, l, acc = lax.fori_loop(0, nkb, body, init)
        out = out + acc / l
    o_ref[0] = out


def _fox_attention(q3, k3, v3, cum, cumt, tq, tk):
    B, S, C = q3.shape
    nh = C // HEAD_DIM
    return pl.pallas_call(
        functools.partial(_fox_kernel, tq=tq, tk=tk),
        grid=(B, nh // 2, S // tq),
        in_specs=[pl.BlockSpec((1, tq, LANES), lambda b, p, i: (b, i, p)),
                  pl.BlockSpec((1, S, LANES), lambda b, p, i: (b, 0, p)),
                  pl.BlockSpec((1, S, LANES), lambda b, p, i: (b, 0, p)),
                  pl.BlockSpec((1, tq, LANES), lambda b, p, i: (b, i, 0)),
                  pl.BlockSpec((1, nh, S), lambda b, p, i: (b, 0, 0))],
        out_specs=pl.BlockSpec((1, tq, LANES), lambda b, p, i: (b, i, p)),
        out_shape=jax.ShapeDtypeStruct((B, S, C), F32),
        compiler_params=_cparams(("arbitrary", "arbitrary", "arbitrary")),
        name="fox_attention",
    )(q3, k3, v3, cum, cumt)


def _mix_kernel(yr_ref, g_ref, yf_ref, x_ref, mod_ref, wo_ref, n2_ref, wr_ref,
                x1_ref, h2_ref, lg_ref, *, c):
    m = mod_ref[0]
    a = (yr_ref[...] * g_ref[...]).astype(BF16)
    b = yf_ref[...].astype(BF16)
    mix = (jnp.dot(a, wo_ref[0:c, :], preferred_element_type=F32)
           + jnp.dot(b, wo_ref[c:, :], preferred_element_type=F32))
    x1 = x_ref[...] + m[2:3] * mix
    x1_ref[...] = x1
    h2 = _rms(x1) * n2_ref[...] * (1.0 + m[4:5]) + m[3:4]
    h2_ref[...] = h2.astype(BF16)
    lg_ref[...] = jnp.dot(h2, wr_ref[...], precision=HIGHEST, preferred_element_type=F32)


def _mix(y_rwkv, g, y_fox, x2, mod3, w_out, norm2_g, w_router, S, tm):
    T, D = x2.shape
    c = y_rwkv.shape[1]
    row = lambda i: (i, 0)
    const = lambda i: (0, 0)
    return pl.pallas_call(
        functools.partial(_mix_kernel, c=c),
        grid=(T // tm,),
        in_specs=[pl.BlockSpec((tm, c), row), pl.BlockSpec((tm, c), row), pl.BlockSpec((tm, c), row),
                  pl.BlockSpec((tm, D), row),
                  pl.BlockSpec((1, 6, D), lambda i: ((i * tm) // S, 0, 0)),
                  pl.BlockSpec((D, D), const), pl.BlockSpec((1, D), const),
                  pl.BlockSpec((D, LANES), const)],
        out_specs=[pl.BlockSpec((tm, D), row), pl.BlockSpec((tm, D), row), pl.BlockSpec((tm, LANES), row)],
        out_shape=[jax.ShapeDtypeStruct((T, D), F32), jax.ShapeDtypeStruct((T, D), BF16),
                   jax.ShapeDtypeStruct((T, LANES), F32)],
        compiler_params=_cparams(("arbitrary",)),
        name="out_proj_norm2_router",
    )(y_rwkv, g, y_fox, x2, mod3, w_out, norm2_g, w_router)


E_ROW0 = 8


def _first_argmax(vals, n):
    mx = jnp.max(vals, axis=0, keepdims=True)
    idx = lax.broadcasted_iota(I32, vals.shape, 0).astype(F32)
    first = jnp.min(jnp.where(vals == mx, idx, float(n)), axis=0, keepdims=True)
    return first.astype(I32), mx


def _route_kernel(lg_ref, bias_ref, tri_ref, ids_ref, wtok_ref, cnt_ref, carry):
    @pl.when(pl.program_id(0) == 0)
    def _():
        carry[...] = jnp.zeros_like(carry)

    lt = (lg_ref[...] + bias_ref[...]).T
    tm = lt.shape[1]
    grp = lt[0:N_GROUPS]
    ge = jnp.exp(grp - jnp.max(grp, axis=0, keepdims=True))
    gp = ge / jnp.sum(ge, axis=0, keepdims=True)
    g_sel, p_g = _first_argmax(gp, N_GROUPS)
    sel = jnp.zeros((EXPERTS_PER_GROUP, tm), F32)
    for g in range(N_GROUPS):
        lo = E_ROW0 + g * EXPERTS_PER_GROUP
        sel = jnp.where(g_sel == g, lt[lo:lo + EXPERTS_PER_GROUP], sel)
    ee = jnp.exp(sel - jnp.max(sel, axis=0, keepdims=True))
    ep = ee / jnp.sum(ee, axis=0, keepdims=True)
    i0, p0 = _first_argmax(ep, EXPERTS_PER_GROUP)
    idx8 = lax.broadcasted_iota(I32, ep.shape, 0)
    i1, p1 = _first_argmax(jnp.where(idx8 == i0, -1.0, ep), EXPERTS_PER_GROUP)
    den = p0 + p1
    w0 = p_g * p0 / den
    w1 = p_g * p1 / den
    e0 = g_sel * EXPERTS_PER_GROUP + i0
    e1 = g_sel * EXPERTS_PER_GROUP + i1

    ide = lax.broadcasted_iota(I32, (N_EXPERTS, tm), 0)
    oh0 = ide == e0
    oh1 = ide == e1
    oh = oh0.astype(F32) + oh1.astype(F32)
    incl = jnp.dot(oh.astype(BF16), tri_ref[...], preferred_element_type=F32)
    base = carry[...] + (incl - oh)
    r0 = jnp.sum(jnp.where(oh0, base, 0.0), axis=0, keepdims=True)
    r1 = jnp.sum(jnp.where(oh1, base, 0.0), axis=0, keepdims=True)
    carry[...] = carry[...] + incl[:, tm - 1:tm]
    cnt_ref[...] = jnp.broadcast_to(carry[...], cnt_ref.shape)
    ids_ref[...] = jnp.concatenate(
        [e0, e1, r0.astype(I32), r1.astype(I32), jnp.zeros((4, tm), I32)], axis=0)
    wtok_ref[...] = jnp.concatenate([w0, w1, jnp.zeros((LANES - 2, tm), F32)], axis=0).T


def _route(logits, bias_row, tm):
    T = logits.shape[0]
    tri = (lax.broadcasted_iota(I32, (tm, tm), 0) <= lax.broadcasted_iota(I32, (tm, tm), 1)).astype(BF16)
    return pl.pallas_call(
        _route_kernel,
        grid=(T // tm,),
        in_specs=[pl.BlockSpec((tm, LANES), lambda i: (i, 0)),
                  pl.BlockSpec((1, LANES), lambda i: (0, 0)),
                  pl.BlockSpec((tm, tm), lambda i: (0, 0))],
        out_specs=[pl.BlockSpec((8, tm), lambda i: (0, i)),
                   pl.BlockSpec((tm, LANES), lambda i: (i, 0)),
                   pl.BlockSpec((N_EXPERTS, LANES), lambda i: (0, 0))],
        out_shape=[jax.ShapeDtypeStruct((8, T), I32), jax.ShapeDtypeStruct((T, LANES), F32),
                   jax.ShapeDtypeStruct((N_EXPERTS, LANES), F32)],
        scratch_shapes=[pltpu.VMEM((N_EXPERTS, 1), F32)],
        compiler_params=_cparams(("arbitrary",)),
        name="route_rank",
    )(logits, bias_row, tri)


def _row_copy(src, dst, sem):
    return pltpu.make_async_copy(src, dst, sem)


def _dispatch_kernel(ps_ref, ids_ref, h_ref, xs_in, xs_out, sem):
    del xs_in
    tmd = h_ref.shape[0]

    def issue(j, c):
        d0 = ps_ref[ids_ref[0, j]] + ids_ref[2, j]
        d1 = ps_ref[ids_ref[1, j]] + ids_ref[3, j]
        _row_copy(h_ref.at[pl.ds(j, 1)], xs_out.at[pl.ds(d0, 1)], sem).start()
        _row_copy(h_ref.at[pl.ds(j, 1)], xs_out.at[pl.ds(d1, 1)], sem).start()
        return c

    lax.fori_loop(0, tmd, issue, 0)

    def drain(j, c):
        _row_copy(h_ref.at[pl.ds(0, 1)], xs_out.at[pl.ds(0, 1)], sem).wait()
        _row_copy(h_ref.at[pl.ds(0, 1)], xs_out.at[pl.ds(0, 1)], sem).wait()
        return c

    lax.fori_loop(0, tmd, drain, 0)


def _dispatch(pstarts, ids, h2_words, xs_zero, tmd):
    T, W = h2_words.shape
    P = xs_zero.shape[0]
    grid_spec = pltpu.PrefetchScalarGridSpec(
        num_scalar_prefetch=1,
        grid=(T // tmd,),
        in_specs=[pl.BlockSpec((8, tmd), lambda i, ps: (0, i), memory_space=pltpu.SMEM),
                  pl.BlockSpec((tmd, W), lambda i, ps: (i, 0)),
                  pl.BlockSpec(memory_space=pl.ANY)],
        out_specs=pl.BlockSpec(memory_space=pl.ANY),
        scratch_shapes=[pltpu.SemaphoreType.DMA(())],
    )
    return pl.pallas_call(
        _dispatch_kernel,
        grid_spec=grid_spec,
        out_shape=jax.ShapeDtypeStruct((P, W), h2_words.dtype),
        input_output_aliases={3: 0},
        compiler_params=_cparams(("arbitrary",)),
        name="moe_dispatch",
    )(pstarts, ids, h2_words, xs_zero)


def _expert_kernel(be_ref, nu_ref, x_ref, wg_ref, wu_ref, wd_ref, y_ref):
    del be_ref
    i = pl.program_id(0)

    @pl.when(i < nu_ref[0])
    def _():
        x = x_ref[...]
        g = jnp.dot(x, wg_ref[0], preferred_element_type=F32)
        u = jnp.dot(x, wu_ref[0], preferred_element_type=F32)
        hid = (g * _sigmoid(g) * u).astype(BF16)
        y_ref[...] = jnp.dot(hid, wd_ref[0], preferred_element_type=F32)

    @pl.when(i >= nu_ref[0])
    def _():
        y_ref[...] = jnp.zeros_like(y_ref)


def _experts(blk_e, n_used, xs, wg, wu, wd, tme):
    P, D = xs.shape
    F = wg.shape[2]
    grid_spec = pltpu.PrefetchScalarGridSpec(
        num_scalar_prefetch=2,
        grid=(P // tme,),
        in_specs=[pl.BlockSpec((tme, D), lambda i, be, nu: (i, 0)),
                  pl.BlockSpec((1, D, F), lambda i, be, nu: (be[i], 0, 0)),
                  pl.BlockSpec((1, D, F), lambda i, be, nu: (be[i], 0, 0)),
                  pl.BlockSpec((1, F, D), lambda i, be, nu: (be[i], 0, 0))],
        out_specs=pl.BlockSpec((tme, D), lambda i, be, nu: (i, 0)),
    )
    return pl.pallas_call(
        _expert_kernel,
        grid_spec=grid_spec,
        out_shape=jax.ShapeDtypeStruct((P, D), F32),
        compiler_params=_cparams(("arbitrary",)),
        name="moe_experts",
    )(blk_e, n_used, xs, wg, wu, wd)


def _combine_kernel(ps_ref, ids_ref, ys_ref, wtok_ref, x1_ref, mod_ref, gf_ref, o_ref, ybuf, sem):
    tmc = x1_ref.shape[0]

    def issue(j, c):
        d0 = ps_ref[ids_ref[0, j]] + ids_ref[2, j]
        d1 = ps_ref[ids_ref[1, j]] + ids_ref[3, j]
        _row_copy(ys_ref.at[pl.ds(d0, 1)], ybuf.at[0, pl.ds(j, 1)], sem).start()
        _row_copy(ys_ref.at[pl.ds(d1, 1)], ybuf.at[1, pl.ds(j, 1)], sem).start()
        return c

    lax.fori_loop(0, tmc, issue, 0)

    def drain(j, c):
        _row_copy(ys_ref.at[pl.ds(0, 1)], ybuf.at[0, pl.ds(0, 1)], sem).wait()
        _row_copy(ys_ref.at[pl.ds(0, 1)], ybuf.at[1, pl.ds(0, 1)], sem).wait()
        return c

    lax.fori_loop(0, tmc, drain, 0)

    m = mod_ref[0]
    w = wtok_ref[...]
    ff = w[:, 0:1] * ybuf[0] + w[:, 1:2] * ybuf[1]
    x2 = x1_ref[...] + m[5:6] * ff
    o_ref[...] = _rms(x2) * gf_ref[...]


def _combine(pstarts, ids, ys, wtok, x1, mod3, norm_f_g, S, tmc):
    T, D = x1.shape
    grid_spec = pltpu.PrefetchScalarGridSpec(
        num_scalar_prefetch=1,
        grid=(T // tmc,),
        in_specs=[pl.BlockSpec((8, tmc), lambda i, ps: (0, i), memory_space=pltpu.SMEM),
                  pl.BlockSpec(memory_space=pl.ANY),
                  pl.BlockSpec((tmc, LANES), lambda i, ps: (i, 0)),
                  pl.BlockSpec((tmc, D), lambda i, ps: (i, 0)),
                  pl.BlockSpec((1, 6, D), lambda i, ps: ((i * tmc) // S, 0, 0)),
                  pl.BlockSpec((1, D), lambda i, ps: (0, 0))],
        out_specs=pl.BlockSpec((tmc, D), lambda i, ps: (i, 0)),
        scratch_shapes=[pltpu.VMEM((2, tmc, D), F32), pltpu.SemaphoreType.DMA(())],
    )
    return pl.pallas_call(
        _combine_kernel,
        grid_spec=grid_spec,
        out_shape=jax.ShapeDtypeStruct((T, D), F32),
        compiler_params=_cparams(("arbitrary",)),
        name="moe_combine_final_norm",
    )(pstarts, ids, ys, wtok, x1, mod3, norm_f_g)


def _pick(n, pref):
    t = min(pref, n)
    while n % t:
        t //= 2
    return t


def _layer(x, mod3, norm1_g, w_in, rwkv_mu, rwkv_w0, rwkv_w_up, rwkv_a0, rwkv_a_up, rwkv_g_up, rwkv_k_k,
           rwkv_k_a, rwkv_r_k, rwkv_lnx_g, rwkv_lnx_b, fox_f_bias, w_out, norm2_g, moe_w_grp, moe_b_grp,
           moe_w_rt, moe_b_rt, moe_w_gate, moe_w_up, moe_w_down):
    B, S, D = x.shape
    T = B * S
    c = rwkv_w0.shape[0]
    nh_r = c // HEAD_DIM
    n_rwkv = rwkv_mu.shape[0]
    nh_f = fox_f_bias.shape[0]
    c_fox = nh_f * HEAD_DIM
    d_lora = rwkv_w_up.shape[0]
    x2 = x.reshape(T, D)

    w_pad = jnp.pad(w_in, ((0, 0), (0, LANES - nh_f))).astype(BF16)
    p_rwkv, q, k, v, f = _project(x2, mod3, norm1_g.reshape(1, D), w_pad, S, n_rwkv, c_fox, _pick(T, 256))

    wup_pad = jnp.pad(rwkv_w_up, ((0, LANES - d_lora), (0, 0)))
    aup_pad = jnp.pad(rwkv_a_up, ((d_lora, LANES - d_lora - rwkv_a_up.shape[0]), (0, 0)))
    r_t, k_t, v_t, w_t, a_t, g_t = _rwkv_prep(
        p_rwkv, rwkv_mu.reshape(1, n_rwkv), rwkv_w0.reshape(1, c), wup_pad, rwkv_a0.reshape(1, c), aup_pad,
        rwkv_g_up, B, S, c, _pick(S, 256))
    inst = B * nh_r
    to_scan = lambda t: t.reshape(B, S, nh_r, HEAD_DIM).transpose(1, 3, 0, 2).reshape(S, HEAD_DIM, inst)
    per_inst = lambda p: jnp.tile(p.reshape(nh_r, HEAD_DIM).T, (1, B))
    y_scan = _wkv_scan(to_scan(r_t), to_scan(k_t), to_scan(v_t), to_scan(w_t), to_scan(a_t),
                       per_inst(rwkv_k_k), per_inst(rwkv_k_a), per_inst(rwkv_r_k), per_inst(rwkv_lnx_g),
                       per_inst(rwkv_lnx_b), _pick(S, 32))
    y_rwkv = y_scan.reshape(S, HEAD_DIM, B, nh_r).transpose(2, 0, 3, 1).reshape(T, c)

    fb_pad = jnp.pad(fox_f_bias, (0, LANES - nh_f)).reshape(1, LANES)
    cum, cumt = _forget_cumsum(f.reshape(B, S, LANES), fb_pad, nh_f, _pick(S, 512))
    blk = _pick(S, 256)
    y_fox = _fox_attention(q.reshape(B, S, c_fox), k.reshape(B, S, c_fox), v.reshape(B, S, c_fox),
                           cum, cumt, blk, blk).reshape(T, c_fox)

    w_router = jnp.zeros((D, LANES), F32)
    w_router = w_router.at[:, 0:N_GROUPS].set(moe_w_grp).at[:, E_ROW0:E_ROW0 + N_EXPERTS].set(moe_w_rt)
    b_router = jnp.zeros((1, LANES), F32)
    b_router = b_router.at[0, 0:N_GROUPS].set(moe_b_grp).at[0, E_ROW0:E_ROW0 + N_EXPERTS].set(moe_b_rt)
    x1, h2, logits = _mix(y_rwkv, g_t, y_fox, x2, mod3, w_out.astype(BF16), norm2_g.reshape(1, D),
                          w_router, S, _pick(T, 256))

    ids, wtok, cnt = _route(logits, b_router, _pick(T, 1024))
    counts = cnt[:, 0].astype(I32)
    padded = (counts + EXPERT_ROWS - 1) // EXPERT_ROWS * EXPERT_ROWS
    pends = jnp.cumsum(padded)
    pstarts = (pends - padded).astype(I32)
    n_rows = (2 * T + N_EXPERTS * (EXPERT_ROWS - 1) + EXPERT_ROWS - 1) // EXPERT_ROWS * EXPERT_ROWS
    n_blocks = n_rows // EXPERT_ROWS
    blk_e = jnp.searchsorted(pends, jnp.arange(n_blocks, dtype=pends.dtype) * EXPERT_ROWS, side="right")
    blk_e = jnp.minimum(blk_e, N_EXPERTS - 1).astype(I32)
    n_used = (pends[-1:] // EXPERT_ROWS).astype(I32)

    h2_words = lax.bitcast_convert_type(h2.reshape(T, D // 2, 2), jnp.uint32)
    xs_words = _dispatch(pstarts, ids, h2_words, jnp.zeros((n_rows, D // 2), jnp.uint32), _pick(T, 256))
    xs = lax.bitcast_convert_type(xs_words, BF16).reshape(n_rows, D)
    ys = _experts(blk_e, n_used, xs, moe_w_gate.astype(BF16), moe_w_up.astype(BF16), moe_w_down.astype(BF16),
                  EXPERT_ROWS)
    return x1, pstarts, ids, ys, wtok


def kernel(x, c, w_ada, b_ada, norm1_g, w_in, rwkv_mu, rwkv_w0, rwkv_w_up, rwkv_a0, rwkv_a_up, rwkv_g_up, rwkv_k_k, rwkv_k_a, rwkv_r_k, rwkv_lnx_g, rwkv_lnx_b, fox_f_bias, w_out, norm2_g, moe_w_grp, moe_b_grp, moe_w_rt, moe_b_rt, moe_w_gate, moe_w_up, moe_w_down, norm_f_g):
    B, S, D = x.shape
    assert w_ada.shape[0] == 1, "single-layer model"
    mod3 = _adaln_mod(c, w_ada[0], b_ada[0]).reshape(B, 6, D)
    x1, pstarts, ids, ys, wtok = _layer(
        x, mod3, norm1_g[0], w_in[0], rwkv_mu[0], rwkv_w0[0], rwkv_w_up[0], rwkv_a0[0], rwkv_a_up[0],
        rwkv_g_up[0], rwkv_k_k[0], rwkv_k_a[0], rwkv_r_k[0], rwkv_lnx_g[0], rwkv_lnx_b[0], fox_f_bias[0],
        w_out[0], norm2_g[0], moe_w_grp[0], moe_b_grp[0], moe_w_rt[0], moe_b_rt[0], moe_w_gate[0],
        moe_w_up[0], moe_w_down[0])
    out = _combine(pstarts, ids, ys, wtok, x1, mod3, norm_f_g.reshape(1, D), S, _pick(B * S, 256))
    return out.reshape(B, S, D)
```

```python
import functools

import jax
import jax.numpy as jnp
from jax import lax
from jax.experimental import pallas as pl
from jax.experimental.pallas import tpu as pltpu

F32 = jnp.float32
BF16 = jnp.bfloat16
I32 = jnp.int32
HIGHEST = lax.Precision.HIGHEST

HEAD_DIM = 64
N_GROUPS = 4
EXPERTS_PER_GROUP = 8
N_EXPERTS = N_GROUPS * EXPERTS_PER_GROUP
NORM_EPS = 1e-6
GN_EPS = 64e-5
LANES = 128
VMEM_LIMIT = 56 * 1024 * 1024

EXPERT_ROWS = 256
MASK_VALUE = -1e30


def _cparams(sem):
    return pltpu.CompilerParams(dimension_semantics=sem, vmem_limit_bytes=VMEM_LIMIT)


def _sigmoid(x):
    return 1.0 / (1.0 + jnp.exp(-x))


def _softplus(x):
    return jnp.maximum(x, 0.0) + jnp.log(1.0 + jnp.exp(-jnp.abs(x)))


def _rms(x):
    return x * lax.rsqrt(jnp.mean(x * x, axis=-1, keepdims=True) + NORM_EPS)


def _mod_kernel(c_ref, w_ref, b_ref, o_ref):
    c = c_ref[...]
    cond = c * _sigmoid(c)
    o_ref[...] = jnp.dot(cond, w_ref[...], precision=HIGHEST, preferred_element_type=F32) + b_ref[...]


def _adaln_mod(c, w_ada, b_ada):
    B, D = c.shape
    N = w_ada.shape[1]
    tn = D
    return pl.pallas_call(
        _mod_kernel,
        grid=(N // tn,),
        in_specs=[pl.BlockSpec((B, D), lambda j: (0, 0)),
                  pl.BlockSpec((D, tn), lambda j: (0, j)),
                  pl.BlockSpec((1, tn), lambda j: (0, j))],
        out_specs=pl.BlockSpec((B, tn), lambda j: (0, j)),
        out_shape=jax.ShapeDtypeStruct((B, N), F32),
        compiler_params=_cparams(("arbitrary",)),
        name="adaln_mod",
    )(c, w_ada, b_ada.reshape(1, N))


def _proj_kernel(x_ref, mod_ref, g_ref, w_ref, pr_ref, q_ref, k_ref, v_ref, f_ref, *, n_rwkv, c_fox):
    m = mod_ref[0]
    h = (_rms(x_ref[...]) * g_ref[...] * (1.0 + m[1:2]) + m[0:1]).astype(BF16)
    o = n_rwkv
    pr_ref[...] = jnp.dot(h, w_ref[:, 0:o], preferred_element_type=F32)
    q = jnp.dot(h, w_ref[:, o:o + c_fox], preferred_element_type=F32)
    q_ref[...] = (q * (LOG2E * HEAD_DIM ** -0.5)).astype(BF16)
    k_ref[...] = jnp.dot(h, w_ref[:, o + c_fox:o + 2 * c_fox], preferred_element_type=F32).astype(BF16)
    v_ref[...] = jnp.dot(h, w_ref[:, o + 2 * c_fox:o + 3 * c_fox], preferred_element_type=F32).astype(BF16)
    f_ref[...] = jnp.dot(h, w_ref[:, o + 3 * c_fox:o + 3 * c_fox + LANES], preferred_element_type=F32)


def _project(x2, mod3, norm_g, w_pad, S, n_rwkv, c_fox, tm):
    T, D = x2.shape
    NW = w_pad.shape[1]
    row = lambda i: (i, 0)
    return pl.pallas_call(
        functools.partial(_proj_kernel, n_rwkv=n_rwkv, c_fox=c_fox),
        grid=(T // tm,),
        in_specs=[pl.BlockSpec((tm, D), row),
                  pl.BlockSpec((1, 6, D), lambda i: ((i * tm) // S, 0, 0)),
                  pl.BlockSpec((1, D), lambda i: (0, 0)),
                  pl.BlockSpec((D, NW), lambda i: (0, 0))],
        out_specs=[pl.BlockSpec((tm, n_rwkv), row), pl.BlockSpec((tm, c_fox), row),
                   pl.BlockSpec((tm, c_fox), row), pl.BlockSpec((tm, c_fox), row),
                   pl.BlockSpec((tm, LANES), row)],
        out_shape=[jax.ShapeDtypeStruct((T, n_rwkv), F32), jax.ShapeDtypeStruct((T, c_fox), BF16),
                   jax.ShapeDtypeStruct((T, c_fox), BF16), jax.ShapeDtypeStruct((T, c_fox), BF16),
                   jax.ShapeDtypeStruct((T, LANES), F32)],
        compiler_params=_cparams(("arbitrary",)),
        name="in_proj",
    )(x2, mod3, norm_g, w_pad)


def _rwkv_prep_kernel(p_ref, mu_ref, w0_ref, wup_ref, a0_ref, aup_ref, gup_ref,
                      r_ref, k_ref, v_ref, w_ref, a_ref, g_ref, carry, *, c):
    @pl.when(pl.program_id(1) == 0)
    def _():
        carry[...] = jnp.zeros_like(carry)

    p = p_ref[...]
    tt = p.shape[0]
    row = lax.broadcasted_iota(I32, (tt, 1), 0)
    prev = jnp.where(row == 0, carry[...], pltpu.roll(p, 1, 0))
    carry[...] = p[tt - 1:tt, :]
    xs = p + (prev - p) * mu_ref[...]
    r_ref[...] = xs[:, 0:c]
    k_ref[...] = xs[:, c:2 * c]
    v_ref[...] = xs[:, 2 * c:3 * c]
    lo = xs[:, 3 * c:3 * c + LANES]
    wl = jnp.dot(jnp.tanh(lo), wup_ref[...], precision=HIGHEST, preferred_element_type=F32)
    w_log = -_softplus(-(w0_ref[...] + wl)) - 0.5
    w_ref[...] = jnp.exp(-jnp.exp(w_log))
    al = jnp.dot(lo, aup_ref[...], precision=HIGHEST, preferred_element_type=F32)
    a_ref[...] = _sigmoid(a0_ref[...] + al)
    gd = _sigmoid(xs[:, 3 * c + LANES:3 * c + 2 * LANES])
    g_ref[...] = jnp.dot(gd, gup_ref[...], precision=HIGHEST, preferred_element_type=F32)


def _rwkv_prep(p_rwkv, mu, w0, wup_pad, a0, aup_pad, gup, B, S, c, tt):
    T, NR = p_rwkv.shape
    nt = S // tt
    row = lambda b, i: (b * nt + i, 0)
    const = lambda b, i: (0, 0)
    out = jax.ShapeDtypeStruct((T, c), F32)
    return pl.pallas_call(
        functools.partial(_rwkv_prep_kernel, c=c),
        grid=(B, nt),
        in_specs=[pl.BlockSpec((tt, NR), row), pl.BlockSpec((1, NR), const),
                  pl.BlockSpec((1, c), const), pl.BlockSpec((LANES, c), const),
                  pl.BlockSpec((1, c), const), pl.BlockSpec((LANES, c), const),
                  pl.BlockSpec((LANES, c), const)],
        out_specs=[pl.BlockSpec((tt, c), row)] * 6,
        out_shape=[out] * 6,
        scratch_shapes=[pltpu.VMEM((1, NR), F32)],
        compiler_params=_cparams(("arbitrary", "arbitrary")),
        name="rwkv_prep",
    )(p_rwkv, mu, w0, wup_pad, a0, aup_pad, gup)


ROW_UNROLL = 4


def _scan_kernel(r_ref, k_ref, v_ref, w_ref, a_ref, kkc_ref, kac_ref, rkc_ref, lg_ref, lb_ref,
                 y_ref, st, kk_s, b_s, km_s, yraw):
    @pl.when(pl.program_id(0) == 0)
    def _():
        st[...] = jnp.zeros_like(st)

    tc, n, _ = r_ref.shape
    k = k_ref[...]
    a = a_ref[...]
    kkr = k * kkc_ref[...][None]
    nrm = jnp.sqrt(jnp.sum(kkr * kkr, axis=1, keepdims=True))
    kk = kkr / jnp.maximum(nrm, 1e-12)
    kk_s[...] = kk
    b_s[...] = kk * a
    km_s[...] = k * (1.0 + (a - 1.0) * kac_ref[...][None])

    def step(t, carry):
        kk_t = kk_s[t]
        w_t = w_ref[t]
        b_t = b_s[t]
        k_t = km_s[t]
        r_t = r_ref[t]

        def rows(j, c2):
            for u in range(ROW_UNROLL):
                vv = j * ROW_UNROLL + u
                s_old = st[vv]
                sa = -jnp.sum(s_old * kk_t, axis=0, keepdims=True)
                s_new = s_old * w_t + sa * b_t + v_ref[t, pl.ds(vv, 1), :] * k_t
                st[vv] = s_new
                yraw[t, pl.ds(vv, 1), :] = jnp.sum(s_new * r_t, axis=0, keepdims=True)
            return c2

        lax.fori_loop(0, n // ROW_UNROLL, rows, 0)
        return carry

    lax.fori_loop(0, tc, step, 0)

    y = yraw[...]
    mean = jnp.mean(y, axis=1, keepdims=True)
    yc = y - mean
    var = jnp.mean(yc * yc, axis=1, keepdims=True)
    yn = yc * lax.rsqrt(var + GN_EPS) * lg_ref[...][None] + lb_ref[...][None]
    bonus = jnp.sum(r_ref[...] * km_s[...] * rkc_ref[...][None], axis=1, keepdims=True) * v_ref[...]
    y_ref[...] = yn + bonus


def _wkv_scan(r, k, v, w, a, kkc, kac, rkc, lg, lb, tc):
    S, n, L = r.shape
    blk = pl.BlockSpec((tc, n, L), lambda i: (i, 0, 0))
    cst = pl.BlockSpec((n, L), lambda i: (0, 0))
    return pl.pallas_call(
        _scan_kernel,
        grid=(S // tc,),
        in_specs=[blk] * 5 + [cst] * 5,
        out_specs=blk,
        out_shape=jax.ShapeDtypeStruct((S, n, L), F32),
        scratch_shapes=[pltpu.VMEM((n, n, L), F32)] + [pltpu.VMEM((tc, n, L), F32)] * 4,
        compiler_params=_cparams(("arbitrary",)),
        name="wkv_scan",
    )(r, k, v, w, a, kkc, kac, rkc, lg, lb)


LOG2E = 1.4426950408889634
N_PIECES = 3


def _aux_base(h):
    return (1 - h % 2) * HEAD_DIM


def _fox_prep_kernel(q_ref, k_ref, v_ref, f_ref, fb_ref, tri_ref, selq_ref, selk_ref, cq_ref, ck_ref,
                     qa_ref, ka_ref, vt_ref, carry, *, nh):
    @pl.when(pl.program_id(1) == 0)
    def _():
        carry[...] = jnp.zeros_like(carry)

    z = f_ref[...] + fb_ref[...]
    lf = -_softplus(-z)
    inc = jnp.dot(tri_ref[...], lf, precision=HIGHEST, preferred_element_type=F32) + carry[...]
    ts = inc.shape[0]
    carry[...] = inc[ts - 1:ts, :]

    rest = inc * LOG2E
    pieces = []
    for _ in range(N_PIECES):
        piece = rest.astype(BF16)
        pieces.append(piece)
        rest = rest - piece.astype(F32)
    pcs = jnp.concatenate(pieces, axis=1)
    aux_q = jnp.dot(pcs, selq_ref[...], preferred_element_type=F32) + cq_ref[...]
    aux_k = jnp.dot(pcs, selk_ref[...], preferred_element_type=F32) + ck_ref[...]

    lane = lax.broadcasted_iota(I32, (1, nh * LANES), 1)
    own = ((lane // HEAD_DIM) % 2) == ((lane // LANES) % 2)
    q = q_ref[...]
    k = k_ref[...]
    dup = lambda t: jnp.concatenate(
        [t[:, (h // 2) * LANES:(h // 2 + 1) * LANES] for h in range(nh)], axis=1)
    qa_ref[...] = jnp.where(own, dup(q), aux_q.astype(BF16))
    ka_ref[...] = jnp.where(own, dup(k), aux_k.astype(BF16))

    vt = v_ref[...].astype(F32).T
    row = lax.broadcasted_iota(I32, (LANES, 1), 0)
    groups = []
    for h in range(nh):
        pair_rows = vt[(h // 2) * LANES:(h // 2 + 1) * LANES]
        own_rows = (row // HEAD_DIM) == (h % 2)
        ones_row = (row == _aux_base(h)).astype(F32)
        groups.append(jnp.where(own_rows, pair_rows, ones_row))
    vt_ref[0] = jnp.concatenate(groups, axis=0).astype(BF16)


def _fox_prep(q, k, v, f, fb_pad, B, S, nh, ts):
    T, C = q.shape
    nt = S // ts
    G = nh * LANES
    tri = (lax.broadcasted_iota(I32, (ts, ts), 0) >= lax.broadcasted_iota(I32, (ts, ts), 1)).astype(F32)
    selq = jnp.zeros((N_PIECES * LANES, G), F32)
    selk = jnp.zeros((N_PIECES * LANES, G), F32)
    cq = jnp.zeros((1, G), F32)
    ck = jnp.zeros((1, G), F32)
    for h in range(nh):
        base = h * LANES + _aux_base(h)
        for pc in range(N_PIECES):
            selq = selq.at[pc * LANES + h, base + pc].set(1.0)
            selk = selk.at[pc * LANES + h, base + N_PIECES + pc].set(-1.0)
            ck = ck.at[0, base + pc].set(1.0)
            cq = cq.at[0, base + N_PIECES + pc].set(1.0)
    row = lambda b, i: (b * nt + i, 0)
    const = lambda b, i: (0, 0)
    return pl.pallas_call(
        functools.partial(_fox_prep_kernel, nh=nh),
        grid=(B, nt),
        in_specs=[pl.BlockSpec((ts, C), row), pl.BlockSpec((ts, C), row), pl.BlockSpec((ts, C), row),
                  pl.BlockSpec((ts, LANES), row), pl.BlockSpec((1, LANES), const),
                  pl.BlockSpec((ts, ts), const),
                  pl.BlockSpec((N_PIECES * LANES, G), const), pl.BlockSpec((N_PIECES * LANES, G), const),
                  pl.BlockSpec((1, G), const), pl.BlockSpec((1, G), const)],
        out_specs=[pl.BlockSpec((ts, G), row), pl.BlockSpec((ts, G), row),
                   pl.BlockSpec((1, G, ts), lambda b, i: (b, 0, i))],
        out_shape=[jax.ShapeDtypeStruct((T, G), BF16), jax.ShapeDtypeStruct((T, G), BF16),
                   jax.ShapeDtypeStruct((B, G, S), BF16)],
        scratch_shapes=[pltpu.VMEM((1, LANES), F32)],
        compiler_params=_cparams(("arbitrary", "arbitrary")),
        name="fox_prep",
    )(q, k, v, f, fb_pad, tri, selq.astype(BF16), selk.astype(BF16), cq, ck)


def _fox_kernel(q_ref, k_ref, vt_ref, o_ref, acc, mrow, s_a, s_b, lim, *, tq):
    qi = pl.program_id(2)
    n = qi + 1
    acc[...] = jnp.zeros_like(acc)
    mrow[...] = jnp.full_like(mrow, MASK_VALUE)

    @pl.when((pl.program_id(0) == 0) & (pl.program_id(1) == 0) & (qi == 0))
    def _():
        key_le_query = (lax.broadcasted_iota(I32, (tq, tq), 0) <= lax.broadcasted_iota(I32, (tq, tq), 1))
        lim[0] = jnp.full((tq, tq), -MASK_VALUE, F32)
        lim[1] = jnp.where(key_le_query, -MASK_VALUE, MASK_VALUE)
        lim[2] = jnp.full((tq, tq), MASK_VALUE, F32)

    def scores(kb, dst):
        off = pl.multiple_of(jnp.minimum(kb, qi) * tq, tq)
        for hh in range(2):
            grp = slice(hh * LANES, (hh + 1) * LANES)
            dst[hh] = lax.dot_general(k_ref[0, pl.ds(off, tq), grp], q_ref[0, :, grp],
                                      (((1,), (1,)), ((), ())), preferred_element_type=F32)

    def softmax_pv(kb, src):
        off = pl.multiple_of(jnp.minimum(kb, qi) * tq, tq)
        which = (kb >= qi).astype(I32) + (kb > qi).astype(I32)
        for hh in range(2):
            grp = slice(hh * LANES, (hh + 1) * LANES)
            st = jnp.minimum(src[hh], lim[which])
            m_old = mrow[hh]
            m_new = jnp.maximum(m_old, jnp.max(st, axis=0, keepdims=True))
            mrow[hh] = m_new
            p = jnp.exp2(st - m_new).astype(BF16)
            pv = jnp.dot(vt_ref[0, grp, pl.ds(off, tq)], p, preferred_element_type=F32)
            acc[hh] = jnp.exp2(m_old - m_new) * acc[hh] + pv

    scores(0, s_a)

    def trip(j, c):
        scores(2 * j + 1, s_b)
        softmax_pv(2 * j, s_a)
        scores(2 * j + 2, s_a)
        softmax_pv(2 * j + 1, s_b)
        return c

    lax.fori_loop(0, (n + 1) // 2, trip, 0)

    row = lax.broadcasted_iota(I32, (LANES, 1), 0)
    out_t = jnp.zeros((LANES, tq), F32)
    for hh in range(2):
        a = acc[hh]
        base = _aux_base(hh)
        l = a[base:base + 1, :]
        out_t = jnp.where((row // HEAD_DIM) == hh, a / l, out_t)
    o_ref[0] = out_t.T


def _fox_attention(qa, ka, vt, c_fox, tq):
    B, S, G = qa.shape
    nh = G // LANES
    return pl.pallas_call(
        functools.partial(_fox_kernel, tq=tq),
        grid=(B, nh // 2, S // tq),
        in_specs=[pl.BlockSpec((1, tq, 2 * LANES), lambda b, p, i: (b, i, p)),
                  pl.BlockSpec((1, S, 2 * LANES), lambda b, p, i: (b, 0, p)),
                  pl.BlockSpec((1, 2 * LANES, S), lambda b, p, i: (b, p, 0))],
        out_specs=pl.BlockSpec((1, tq, LANES), lambda b, p, i: (b, i, p)),
        out_shape=jax.ShapeDtypeStruct((B, S, c_fox), F32),
        scratch_shapes=[pltpu.VMEM((2, LANES, tq), F32), pltpu.VMEM((2, 1, tq), F32),
                        pltpu.VMEM((2, tq, tq), F32), pltpu.VMEM((2, tq, tq), F32),
                        pltpu.VMEM((3, tq, tq), F32)],
        compiler_params=_cparams(("arbitrary", "arbitrary", "arbitrary")),
        name="fox_attention",
    )(qa, ka, vt)


def _mix_kernel(yr_ref, g_ref, yf_ref, x_ref, mod_ref, wo_ref, n2_ref, wr_ref,
                x1_ref, h2_ref, lg_ref, *, c):
    m = mod_ref[0]
    a = (yr_ref[...] * g_ref[...]).astype(BF16)
    b = yf_ref[...].astype(BF16)
    mix = (jnp.dot(a, wo_ref[0:c, :], preferred_element_type=F32)
           + jnp.dot(b, wo_ref[c:, :], preferred_element_type=F32))
    x1 = x_ref[...] + m[2:3] * mix
    x1_ref[...] = x1
    h2 = _rms(x1) * n2_ref[...] * (1.0 + m[4:5]) + m[3:4]
    h2_ref[...] = _pack_bf16_halves(h2)
    lg_ref[...] = jnp.dot(h2, wr_ref[...], precision=HIGHEST, preferred_element_type=F32)


def _pack_bf16_halves(x):
    w = x.shape[1] // 2
    bits = lax.bitcast_convert_type(x.astype(BF16).astype(F32), jnp.uint32)
    return (bits[:, :w] >> 16) | (bits[:, w:] & jnp.uint32(0xFFFF0000))


def _unpack_bf16_halves(words):
    lo = lax.bitcast_convert_type(words << 16, F32)
    hi = lax.bitcast_convert_type(words & jnp.uint32(0xFFFF0000), F32)
    return jnp.concatenate([lo, hi], axis=1).astype(BF16)


def _mix(y_rwkv, g, y_fox, x2, mod3, w_out, norm2_g, w_router, S, tm):
    T, D = x2.shape
    c = y_rwkv.shape[1]
    row = lambda i: (i, 0)
    const = lambda i: (0, 0)
    return pl.pallas_call(
        functools.partial(_mix_kernel, c=c),
        grid=(T // tm,),
        in_specs=[pl.BlockSpec((tm, c), row), pl.BlockSpec((tm, c), row), pl.BlockSpec((tm, c), row),
                  pl.BlockSpec((tm, D), row),
                  pl.BlockSpec((1, 6, D), lambda i: ((i * tm) // S, 0, 0)),
                  pl.BlockSpec((D, D), const), pl.BlockSpec((1, D), const),
                  pl.BlockSpec((D, LANES), const)],
        out_specs=[pl.BlockSpec((tm, D), row), pl.BlockSpec((tm, D // 2), row), pl.BlockSpec((tm, LANES), row)],
        out_shape=[jax.ShapeDtypeStruct((T, D), F32), jax.ShapeDtypeStruct((T, D // 2), jnp.uint32),
                   jax.ShapeDtypeStruct((T, LANES), F32)],
        compiler_params=_cparams(("arbitrary",)),
        name="out_proj_norm2_router",
    )(y_rwkv, g, y_fox, x2, mod3, w_out, norm2_g, w_router)


E_ROW0 = 8


def _first_argmax(vals, n):
    mx = jnp.max(vals, axis=0, keepdims=True)
    idx = lax.broadcasted_iota(I32, vals.shape, 0).astype(F32)
    first = jnp.min(jnp.where(vals == mx, idx, float(n)), axis=0, keepdims=True)
    return first.astype(I32), mx


def _route_kernel(lg_ref, bias_ref, tri_ref, ids_ref, wtok_ref, cnt_ref, carry):
    @pl.when(pl.program_id(0) == 0)
    def _():
        carry[...] = jnp.zeros_like(carry)

    lt = (lg_ref[...] + bias_ref[...]).T
    tm = lt.shape[1]
    grp = lt[0:N_GROUPS]
    ge = jnp.exp(grp - jnp.max(grp, axis=0, keepdims=True))
    gp = ge / jnp.sum(ge, axis=0, keepdims=True)
    g_sel, p_g = _first_argmax(gp, N_GROUPS)
    sel = jnp.zeros((EXPERTS_PER_GROUP, tm), F32)
    for g in range(N_GROUPS):
        lo = E_ROW0 + g * EXPERTS_PER_GROUP
        sel = jnp.where(g_sel == g, lt[lo:lo + EXPERTS_PER_GROUP], sel)
    ee = jnp.exp(sel - jnp.max(sel, axis=0, keepdims=True))
    ep = ee / jnp.sum(ee, axis=0, keepdims=True)
    i0, p0 = _first_argmax(ep, EXPERTS_PER_GROUP)
    idx8 = lax.broadcasted_iota(I32, ep.shape, 0)
    i1, p1 = _first_argmax(jnp.where(idx8 == i0, -1.0, ep), EXPERTS_PER_GROUP)
    den = p0 + p1
    w0 = p_g * p0 / den
    w1 = p_g * p1 / den
    e0 = g_sel * EXPERTS_PER_GROUP + i0
    e1 = g_sel * EXPERTS_PER_GROUP + i1

    ide = lax.broadcasted_iota(I32, (N_EXPERTS, tm), 0)
    oh0 = ide == e0
    oh1 = ide == e1
    oh = oh0.astype(F32) + oh1.astype(F32)
    incl = jnp.dot(oh.astype(BF16), tri_ref[...], preferred_element_type=F32)
    base = carry[...] + (incl - oh)
    r0 = jnp.sum(jnp.where(oh0, base, 0.0), axis=0, keepdims=True)
    r1 = jnp.sum(jnp.where(oh1, base, 0.0), axis=0, keepdims=True)
    carry[...] = carry[...] + incl[:, tm - 1:tm]
    cnt_ref[...] = jnp.broadcast_to(carry[...], cnt_ref.shape)
    ids_ref[...] = jnp.concatenate(
        [e0, e1, r0.astype(I32), r1.astype(I32), jnp.zeros((4, tm), I32)], axis=0)
    wtok_ref[...] = jnp.concatenate([w0, w1, jnp.zeros((LANES - 2, tm), F32)], axis=0).T


def _route(logits, bias_row, tm):
    T = logits.shape[0]
    tri = (lax.broadcasted_iota(I32, (tm, tm), 0) <= lax.broadcasted_iota(I32, (tm, tm), 1)).astype(BF16)
    return pl.pallas_call(
        _route_kernel,
        grid=(T // tm,),
        in_specs=[pl.BlockSpec((tm, LANES), lambda i: (i, 0)),
                  pl.BlockSpec((1, LANES), lambda i: (0, 0)),
                  pl.BlockSpec((tm, tm), lambda i: (0, 0))],
        out_specs=[pl.BlockSpec((8, tm), lambda i: (0, i)),
                   pl.BlockSpec((tm, LANES), lambda i: (i, 0)),
                   pl.BlockSpec((N_EXPERTS, LANES), lambda i: (0, 0))],
        out_shape=[jax.ShapeDtypeStruct((8, T), I32), jax.ShapeDtypeStruct((T, LANES), F32),
                   jax.ShapeDtypeStruct((N_EXPERTS, LANES), F32)],
        scratch_shapes=[pltpu.VMEM((N_EXPERTS, 1), F32)],
        compiler_params=_cparams(("arbitrary",)),
        name="route_rank",
    )(logits, bias_row, tri)


def _row_copy(src, dst, sem):
    return pltpu.make_async_copy(src, dst, sem)


def _dispatch_kernel(ps_ref, pe_ref, nu_ref, ids_ref, h_ref, xs_out, zbuf, sem, zsem, *, n_blocks, n_tail):
    tmd = h_ref.shape[0]
    er = zbuf.shape[0]

    @pl.when(pl.program_id(0) == 0)
    def _():
        zbuf[...] = jnp.zeros_like(zbuf)

        def pad_copy(e):
            return _row_copy(zbuf, xs_out.at[pl.ds(pl.multiple_of(pe_ref[e] - er, er), er)], zsem)

        def tail_copy(j):
            return _row_copy(zbuf, xs_out.at[pl.ds(pl.multiple_of((nu_ref[0] + j) * er, er), er)], zsem)

        def each(fn):
            def pads(e, c):
                @pl.when(pe_ref[e] > ps_ref[e])
                def _():
                    fn(pad_copy(e))
                return c

            def tails(j, c):
                @pl.when(nu_ref[0] + j < n_blocks)
                def _():
                    fn(tail_copy(j))
                return c

            lax.fori_loop(0, N_EXPERTS, pads, 0)
            lax.fori_loop(0, n_tail, tails, 0)

        each(lambda cp: cp.start())
        each(lambda cp: cp.wait())

    def issue(j, c):
        d0 = ps_ref[ids_ref[0, j]] + ids_ref[2, j]
        d1 = ps_ref[ids_ref[1, j]] + ids_ref[3, j]
        _row_copy(h_ref.at[pl.ds(j, 1)], xs_out.at[pl.ds(d0, 1)], sem).start()
        _row_copy(h_ref.at[pl.ds(j, 1)], xs_out.at[pl.ds(d1, 1)], sem).start()
        return c

    lax.fori_loop(0, tmd, issue, 0)

    def drain(j, c):
        _row_copy(h_ref.at[pl.ds(0, 1)], xs_out.at[pl.ds(0, 1)], sem).wait()
        _row_copy(h_ref.at[pl.ds(0, 1)], xs_out.at[pl.ds(0, 1)], sem).wait()
        return c

    lax.fori_loop(0, tmd, drain, 0)


def _dispatch(pstarts, pends, n_used, ids, h2_words, n_rows, tmd):
    T, W = h2_words.shape
    n_blocks = n_rows // EXPERT_ROWS
    n_tail = n_blocks - (2 * T) // EXPERT_ROWS
    grid_spec = pltpu.PrefetchScalarGridSpec(
        num_scalar_prefetch=3,
        grid=(T // tmd,),
        in_specs=[pl.BlockSpec((8, tmd), lambda i, *_: (0, i), memory_space=pltpu.SMEM),
                  pl.BlockSpec((tmd, W), lambda i, *_: (i, 0))],
        out_specs=pl.BlockSpec(memory_space=pl.ANY),
        scratch_shapes=[pltpu.VMEM((EXPERT_ROWS, W), h2_words.dtype),
                        pltpu.SemaphoreType.DMA(()), pltpu.SemaphoreType.DMA(())],
    )
    return pl.pallas_call(
        functools.partial(_dispatch_kernel, n_blocks=n_blocks, n_tail=n_tail),
        grid_spec=grid_spec,
        out_shape=jax.ShapeDtypeStruct((n_rows, W), h2_words.dtype),
        compiler_params=_cparams(("arbitrary",)),
        name="moe_dispatch",
    )(pstarts, pends, n_used, ids, h2_words)


def _expert_kernel(be_ref, nu_ref, x_ref, wg_ref, wu_ref, wd_ref, y_ref, wg_s, wu_s, wd_s):
    i = pl.program_id(0)
    used = i < nu_ref[0]

    @pl.when(used & ((i == 0) | (be_ref[i] != be_ref[jnp.maximum(i - 1, 0)])))
    def _():
        wg_s[...] = wg_ref[0].astype(BF16)
        wu_s[...] = wu_ref[0].astype(BF16)
        wd_s[...] = wd_ref[0].astype(BF16)

    @pl.when(used)
    def _():
        x = _unpack_bf16_halves(x_ref[...])
        g = jnp.dot(x, wg_s[...], preferred_element_type=F32)
        u = jnp.dot(x, wu_s[...], preferred_element_type=F32)
        hid = (g * _sigmoid(g) * u).astype(BF16)
        y_ref[...] = jnp.dot(hid, wd_s[...], preferred_element_type=F32)

    @pl.when(jnp.logical_not(used))
    def _():
        y_ref[...] = jnp.zeros_like(y_ref)


def _experts(blk_e, n_used, xs_words, wg, wu, wd, tme):
    P, W = xs_words.shape
    _, D, F = wg.shape
    grid_spec = pltpu.PrefetchScalarGridSpec(
        num_scalar_prefetch=2,
        grid=(P // tme,),
        in_specs=[pl.BlockSpec((tme, W), lambda i, be, nu: (i, 0)),
                  pl.BlockSpec((1, D, F), lambda i, be, nu: (be[i], 0, 0)),
                  pl.BlockSpec((1, D, F), lambda i, be, nu: (be[i], 0, 0)),
                  pl.BlockSpec((1, F, D), lambda i, be, nu: (be[i], 0, 0))],
        out_specs=pl.BlockSpec((tme, D), lambda i, be, nu: (i, 0)),
        scratch_shapes=[pltpu.VMEM((D, F), BF16), pltpu.VMEM((D, F), BF16), pltpu.VMEM((F, D), BF16)],
    )
    return pl.pallas_call(
        _expert_kernel,
        grid_spec=grid_spec,
        out_shape=jax.ShapeDtypeStruct((P, D), F32),
        compiler_params=_cparams(("arbitrary",)),
        name="moe_experts",
    )(blk_e, n_used, xs_words, wg, wu, wd)


def _combine_kernel(ps_ref, ids_ref, ys_ref, wtok_ref, x1_ref, mod_ref, gf_ref, o_ref, ybuf, sem):
    tmc = x1_ref.shape[0]

    def issue(j, c):
        d0 = ps_ref[ids_ref[0, j]] + ids_ref[2, j]
        d1 = ps_ref[ids_ref[1, j]] + ids_ref[3, j]
        _row_copy(ys_ref.at[pl.ds(d0, 1)], ybuf.at[0, pl.ds(j, 1)], sem).start()
        _row_copy(ys_ref.at[pl.ds(d1, 1)], ybuf.at[1, pl.ds(j, 1)], sem).start()
        return c

    lax.fori_loop(0, tmc, issue, 0)

    def drain(j, c):
        _row_copy(ys_ref.at[pl.ds(0, 1)], ybuf.at[0, pl.ds(0, 1)], sem).wait()
        _row_copy(ys_ref.at[pl.ds(0, 1)], ybuf.at[1, pl.ds(0, 1)], sem).wait()
        return c

    lax.fori_loop(0, tmc, drain, 0)

    m = mod_ref[0]
    w = wtok_ref[...]
    ff = w[:, 0:1] * ybuf[0] + w[:, 1:2] * ybuf[1]
    x2 = x1_ref[...] + m[5:6] * ff
    o_ref[...] = _rms(x2) * gf_ref[...]


def _combine(pstarts, ids, ys, wtok, x1, mod3, norm_f_g, S, tmc):
    T, D = x1.shape
    grid_spec = pltpu.PrefetchScalarGridSpec(
        num_scalar_prefetch=1,
        grid=(T // tmc,),
        in_specs=[pl.BlockSpec((8, tmc), lambda i, ps: (0, i), memory_space=pltpu.SMEM),
                  pl.BlockSpec(memory_space=pl.ANY),
                  pl.BlockSpec((tmc, LANES), lambda i, ps: (i, 0)),
                  pl.BlockSpec((tmc, D), lambda i, ps: (i, 0)),
                  pl.BlockSpec((1, 6, D), lambda i, ps: ((i * tmc) // S, 0, 0)),
                  pl.BlockSpec((1, D), lambda i, ps: (0, 0))],
        out_specs=pl.BlockSpec((tmc, D), lambda i, ps: (i, 0)),
        scratch_shapes=[pltpu.VMEM((2, tmc, D), F32), pltpu.SemaphoreType.DMA(())],
    )
    return pl.pallas_call(
        _combine_kernel,
        grid_spec=grid_spec,
        out_shape=jax.ShapeDtypeStruct((T, D), F32),
        compiler_params=_cparams(("arbitrary",)),
        name="moe_combine_final_norm",
    )(pstarts, ids, ys, wtok, x1, mod3, norm_f_g)


def _pick(n, pref):
    t = min(pref, n)
    while n % t:
        t //= 2
    return t


def _layer(x, mod3, norm1_g, w_in, rwkv_mu, rwkv_w0, rwkv_w_up, rwkv_a0, rwkv_a_up, rwkv_g_up, rwkv_k_k,
           rwkv_k_a, rwkv_r_k, rwkv_lnx_g, rwkv_lnx_b, fox_f_bias, w_out, norm2_g, moe_w_grp, moe_b_grp,
           moe_w_rt, moe_b_rt, moe_w_gate, moe_w_up, moe_w_down):
    B, S, D = x.shape
    T = B * S
    c = rwkv_w0.shape[0]
    nh_r = c // HEAD_DIM
    n_rwkv = rwkv_mu.shape[0]
    nh_f = fox_f_bias.shape[0]
    c_fox = nh_f * HEAD_DIM
    d_lora = rwkv_w_up.shape[0]
    x2 = x.reshape(T, D)

    w_pad = jnp.pad(w_in, ((0, 0), (0, LANES - nh_f))).astype(BF16)
    p_rwkv, q, k, v, f = _project(x2, mod3, norm1_g.reshape(1, D), w_pad, S, n_rwkv, c_fox, _pick(T, 256))

    wup_pad = jnp.pad(rwkv_w_up, ((0, LANES - d_lora), (0, 0)))
    aup_pad = jnp.pad(rwkv_a_up, ((d_lora, LANES - d_lora - rwkv_a_up.shape[0]), (0, 0)))
    r_t, k_t, v_t, w_t, a_t, g_t = _rwkv_prep(
        p_rwkv, rwkv_mu.reshape(1, n_rwkv), rwkv_w0.reshape(1, c), wup_pad, rwkv_a0.reshape(1, c), aup_pad,
        rwkv_g_up, B, S, c, _pick(S, 256))
    inst = B * nh_r
    to_scan = lambda t: t.reshape(B, S, nh_r, HEAD_DIM).transpose(1, 3, 0, 2).reshape(S, HEAD_DIM, inst)
    per_inst = lambda p: jnp.tile(p.reshape(nh_r, HEAD_DIM).T, (1, B))
    y_scan = _wkv_scan(to_scan(r_t), to_scan(k_t), to_scan(v_t), to_scan(w_t), to_scan(a_t),
                       per_inst(rwkv_k_k), per_inst(rwkv_k_a), per_inst(rwkv_r_k), per_inst(rwkv_lnx_g),
                       per_inst(rwkv_lnx_b), _pick(S, 32))
    y_rwkv = y_scan.reshape(S, HEAD_DIM, B, nh_r).transpose(2, 0, 3, 1).reshape(T, c)

    fb_pad = jnp.pad(fox_f_bias, (0, LANES - nh_f)).reshape(1, LANES)
    qa, ka, vt = _fox_prep(q, k, v, f, fb_pad, B, S, nh_f, _pick(S, 512))
    G = nh_f * LANES
    y_fox = _fox_attention(qa.reshape(B, S, G), ka.reshape(B, S, G), vt, c_fox, _pick(S, 256)).reshape(T, c_fox)

    w_router = jnp.zeros((D, LANES), F32)
    w_router = w_router.at[:, 0:N_GROUPS].set(moe_w_grp).at[:, E_ROW0:E_ROW0 + N_EXPERTS].set(moe_w_rt)
    b_router = jnp.zeros((1, LANES), F32)
    b_router = b_router.at[0, 0:N_GROUPS].set(moe_b_grp).at[0, E_ROW0:E_ROW0 + N_EXPERTS].set(moe_b_rt)
    x1, h2_words, logits = _mix(y_rwkv, g_t, y_fox, x2, mod3, w_out.astype(BF16), norm2_g.reshape(1, D),
                          w_router, S, _pick(T, 256))

    ids, wtok, cnt = _route(logits, b_router, _pick(T, 1024))
    counts = cnt[:, 0].astype(I32)
    padded = (counts + EXPERT_ROWS - 1) // EXPERT_ROWS * EXPERT_ROWS
    pends = jnp.cumsum(padded).astype(I32)
    pstarts = pends - padded
    n_rows = (2 * T + N_EXPERTS * (EXPERT_ROWS - 1) + EXPERT_ROWS - 1) // EXPERT_ROWS * EXPERT_ROWS
    n_blocks = n_rows // EXPERT_ROWS
    blk_start = jnp.arange(n_blocks, dtype=I32) * EXPERT_ROWS
    blk_e = jnp.sum((pends[None, :] <= blk_start[:, None]).astype(I32), axis=1)
    blk_e = jnp.minimum(blk_e, N_EXPERTS - 1)
    n_used = pends[-1:] // EXPERT_ROWS

    xs_words = _dispatch(pstarts, pends, n_used, ids, h2_words, n_rows, _pick(T, 256))
    ys = _experts(blk_e, n_used, xs_words, moe_w_gate, moe_w_up, moe_w_down, EXPERT_ROWS)
    return x1, pstarts, ids, ys, wtok


def kernel(x, c, w_ada, b_ada, norm1_g, w_in, rwkv_mu, rwkv_w0, rwkv_w_up, rwkv_a0, rwkv_a_up, rwkv_g_up, rwkv_k_k, rwkv_k_a, rwkv_r_k, rwkv_lnx_g, rwkv_lnx_b, fox_f_bias, w_out, norm2_g, moe_w_grp, moe_b_grp, moe_w_rt, moe_b_rt, moe_w_gate, moe_w_up, moe_w_down, norm_f_g):
    B, S, D = x.shape
    assert w_ada.shape[0] == 1, "single-layer model"
    mod3 = _adaln_mod(c, w_ada[0], b_ada[0]).reshape(B, 6, D)
    x1, pstarts, ids, ys, wtok = _layer(
        x, mod3, norm1_g[0], w_in[0], rwkv_mu[0], rwkv_w0[0], rwkv_w_up[0], rwkv_a0[0], rwkv_a_up[0],
        rwkv_g_up[0], rwkv_k_k[0], rwkv_k_a[0], rwkv_r_k[0], rwkv_lnx_g[0], rwkv_lnx_b[0], fox_f_bias[0],
        w_out[0], norm2_g[0], moe_w_grp[0], moe_b_grp[0], moe_w_rt[0], moe_b_rt[0], moe_w_gate[0],
        moe_w_up[0], moe_w_down[0])
    out = _combine(pstarts, ids, ys, wtok, x1, mod3, norm_f_g.reshape(1, D), S, _pick(B * S, 256))
    return out.reshape(B, S, D)
```

```python
import functools

import jax
import jax.numpy as jnp
from jax import lax
from jax.experimental import pallas as pl
from jax.experimental.pallas import tpu as pltpu

F32 = jnp.float32
BF16 = jnp.bfloat16
I32 = jnp.int32
HIGHEST = lax.Precision.HIGHEST

HEAD_DIM = 64
N_GROUPS = 4
EXPERTS_PER_GROUP = 8
N_EXPERTS = N_GROUPS * EXPERTS_PER_GROUP
NORM_EPS = 1e-6
GN_EPS = 64e-5
LANES = 128
VMEM_LIMIT = 56 * 1024 * 1024

EXPERT_ROWS = 256
MASK_VALUE = -1e30


def _cparams(sem):
    return pltpu.CompilerParams(dimension_semantics=sem, vmem_limit_bytes=VMEM_LIMIT)


def _sigmoid(x):
    return 1.0 / (1.0 + jnp.exp(-x))


def _softplus(x):
    return jnp.maximum(x, 0.0) + jnp.log(1.0 + jnp.exp(-jnp.abs(x)))


def _split_bf16(w):
    hi = w.astype(BF16)
    return jnp.stack([hi, (w - hi.astype(F32)).astype(BF16)])


def _dot_split(a, w_ref):
    a_hi = a.astype(BF16)
    a_lo = (a - a_hi.astype(F32)).astype(BF16)
    w_hi = w_ref[0]
    return (jnp.dot(a_hi, w_hi, preferred_element_type=F32) + jnp.dot(a_lo, w_hi, preferred_element_type=F32)
            + jnp.dot(a_hi, w_ref[1], preferred_element_type=F32))


def _rms(x):
    return x * lax.rsqrt(jnp.mean(x * x, axis=-1, keepdims=True) + NORM_EPS)


def _mod_kernel(c_ref, w_ref, b_ref, o_ref):
    c = c_ref[...]
    cond = c * _sigmoid(c)
    o_ref[...] = jnp.dot(cond, w_ref[...], precision=HIGHEST, preferred_element_type=F32) + b_ref[...]


def _adaln_mod(c, w_ada, b_ada):
    B, D = c.shape
    N = w_ada.shape[1]
    tn = D
    return pl.pallas_call(
        _mod_kernel,
        grid=(N // tn,),
        in_specs=[pl.BlockSpec((B, D), lambda j: (0, 0)),
                  pl.BlockSpec((D, tn), lambda j: (0, j)),
                  pl.BlockSpec((1, tn), lambda j: (0, j))],
        out_specs=pl.BlockSpec((B, tn), lambda j: (0, j)),
        out_shape=jax.ShapeDtypeStruct((B, N), F32),
        compiler_params=_cparams(("arbitrary",)),
        name="adaln_mod",
    )(c, w_ada, b_ada.reshape(1, N))


def _proj_kernel(x_ref, mod_ref, g_ref, w_ref, pr_ref, q_ref, k_ref, v_ref, f_ref, *, n_rwkv, c_fox):
    m = mod_ref[0]
    h = (_rms(x_ref[...]) * g_ref[...] * (1.0 + m[1:2]) + m[0:1]).astype(BF16)
    o = n_rwkv
    pr_ref[...] = jnp.dot(h, w_ref[:, 0:o], preferred_element_type=F32)
    q = jnp.dot(h, w_ref[:, o:o + c_fox], preferred_element_type=F32)
    q_ref[...] = (q * (LOG2E * HEAD_DIM ** -0.5)).astype(BF16)
    k_ref[...] = jnp.dot(h, w_ref[:, o + c_fox:o + 2 * c_fox], preferred_element_type=F32).astype(BF16)
    v_ref[...] = jnp.dot(h, w_ref[:, o + 2 * c_fox:o + 3 * c_fox], preferred_element_type=F32).astype(BF16)
    f_ref[...] = jnp.dot(h, w_ref[:, o + 3 * c_fox:o + 3 * c_fox + LANES], preferred_element_type=F32)


def _project(x2, mod3, norm_g, w_pad, S, n_rwkv, c_fox, tm):
    T, D = x2.shape
    NW = w_pad.shape[1]
    row = lambda i: (i, 0)
    return pl.pallas_call(
        functools.partial(_proj_kernel, n_rwkv=n_rwkv, c_fox=c_fox),
        grid=(T // tm,),
        in_specs=[pl.BlockSpec((tm, D), row),
                  pl.BlockSpec((1, 6, D), lambda i: ((i * tm) // S, 0, 0)),
                  pl.BlockSpec((1, D), lambda i: (0, 0)),
                  pl.BlockSpec((D, NW), lambda i: (0, 0))],
        out_specs=[pl.BlockSpec((tm, n_rwkv), row), pl.BlockSpec((tm, c_fox), row),
                   pl.BlockSpec((tm, c_fox), row), pl.BlockSpec((tm, c_fox), row),
                   pl.BlockSpec((tm, LANES), row)],
        out_shape=[jax.ShapeDtypeStruct((T, n_rwkv), F32), jax.ShapeDtypeStruct((T, c_fox), BF16),
                   jax.ShapeDtypeStruct((T, c_fox), BF16), jax.ShapeDtypeStruct((T, c_fox), BF16),
                   jax.ShapeDtypeStruct((T, LANES), F32)],
        compiler_params=_cparams(("arbitrary",)),
        name="in_proj",
    )(x2, mod3, norm_g, w_pad)


def _rwkv_prep_kernel(p_ref, mu_ref, w0_ref, wup_ref, a0_ref, aup_ref, gup_ref,
                      r_ref, k_ref, v_ref, w_ref, a_ref, g_ref, carry, *, c):
    @pl.when(pl.program_id(1) == 0)
    def _():
        carry[...] = jnp.zeros_like(carry)

    p = p_ref[...]
    tt = p.shape[0]
    row = lax.broadcasted_iota(I32, (tt, 1), 0)
    prev = jnp.where(row == 0, carry[...], pltpu.roll(p, 1, 0))
    carry[...] = p[tt - 1:tt, :]
    xs = p + (prev - p) * mu_ref[...]
    r_ref[...] = xs[:, 0:c]
    k_ref[...] = xs[:, c:2 * c]
    v_ref[...] = xs[:, 2 * c:3 * c]
    lo = xs[:, 3 * c:3 * c + LANES]
    wl = _dot_split(jnp.tanh(lo), wup_ref)
    w_log = -_softplus(-(w0_ref[...] + wl)) - 0.5
    w_ref[...] = jnp.exp(-jnp.exp(w_log))
    al = _dot_split(lo, aup_ref)
    a_ref[...] = _sigmoid(a0_ref[...] + al)
    gd = _sigmoid(xs[:, 3 * c + LANES:3 * c + 2 * LANES])
    g_ref[...] = _dot_split(gd, gup_ref)


def _rwkv_prep(p_rwkv, mu, w0, wup_pad, a0, aup_pad, gup, B, S, c, tt):
    T, NR = p_rwkv.shape
    nt = S // tt
    row = lambda b, i: (b * nt + i, 0)
    const = lambda b, i: (0, 0)
    const3 = lambda b, i: (0, 0, 0)
    out = jax.ShapeDtypeStruct((T, c), F32)
    return pl.pallas_call(
        functools.partial(_rwkv_prep_kernel, c=c),
        grid=(B, nt),
        in_specs=[pl.BlockSpec((tt, NR), row), pl.BlockSpec((1, NR), const),
                  pl.BlockSpec((1, c), const), pl.BlockSpec((2, LANES, c), const3),
                  pl.BlockSpec((1, c), const), pl.BlockSpec((2, LANES, c), const3),
                  pl.BlockSpec((2, LANES, c), const3)],
        out_specs=[pl.BlockSpec((tt, c), lambda b, i: (i, b))] * 5 + [pl.BlockSpec((tt, c), row)],
        out_shape=[jax.ShapeDtypeStruct((S, B * c), F32)] * 5 + [out],
        scratch_shapes=[pltpu.VMEM((1, NR), F32)],
        compiler_params=_cparams(("arbitrary", "arbitrary")),
        name="rwkv_prep",
    )(p_rwkv, mu, w0, wup_pad, a0, aup_pad, gup)


ROW_UNROLL = 8


def _scan_kernel(r_ref, k_ref, v_ref, w_ref, a_ref, kkc_ref, kac_ref, rkc_ref, lg_ref, lb_ref,
                 y_ref, st, kk_s, b_s, km_s, rr_s, wprev_s, winc_s, yraw):
    @pl.when(pl.program_id(0) == 0)
    def _():
        st[...] = jnp.zeros_like(st)

    tc, n, lanes = r_ref.shape
    k = k_ref[...]
    a = a_ref[...]
    r = r_ref[...]
    kkr = k * kkc_ref[...][None]
    nrm = jnp.sqrt(jnp.sum(kkr * kkr, axis=1, keepdims=True))
    kk = kkr / jnp.maximum(nrm, 1e-12)
    km = k * (1.0 + (a - 1.0) * kac_ref[...][None])
    y_ref[...] = jnp.sum(r * km * rkc_ref[...][None], axis=1, keepdims=True) * v_ref[...]

    def cumulate(t, wc):
        wprev_s[t] = wc
        wc = wc * w_ref[t]
        winc_s[t] = wc
        return wc

    w_chunk = lax.fori_loop(0, tc, cumulate, jnp.ones((n, lanes), F32))
    winc = winc_s[...]
    inv = 1.0 / winc
    kk_s[...] = -(wprev_s[...] * kk)
    b_s[...] = kk * a * inv
    km_s[...] = km * inv
    rr_s[...] = r * winc

    def step(t, carry):
        kk_t = kk_s[t]
        b_t = b_s[t]
        k_t = km_s[t]
        r_t = rr_s[t]

        def rows(j, c2):
            for u in range(ROW_UNROLL):
                vv = j * ROW_UNROLL + u
                s_old = st[vv]
                sa = jnp.sum(s_old * kk_t, axis=0, keepdims=True)
                s_new = s_old + sa * b_t + v_ref[t, pl.ds(vv, 1), :] * k_t
                st[vv] = s_new
                yraw[t, pl.ds(vv, 1), :] = jnp.sum(s_new * r_t, axis=0, keepdims=True)
            return c2

        lax.fori_loop(0, n // ROW_UNROLL, rows, 0)
        return carry

    lax.fori_loop(0, tc, step, 0)
    st[...] = st[...] * w_chunk[None]

    y = yraw[...]
    mean = jnp.mean(y, axis=1, keepdims=True)
    yc = y - mean
    var = jnp.mean(yc * yc, axis=1, keepdims=True)
    yn = yc * lax.rsqrt(var + GN_EPS) * lg_ref[...][None] + lb_ref[...][None]
    y_ref[...] = y_ref[...] + yn


def _wkv_scan(r, k, v, w, a, kkc, kac, rkc, lg, lb, tc):
    S, n, L = r.shape
    blk = pl.BlockSpec((tc, n, L), lambda i: (i, 0, 0))
    cst = pl.BlockSpec((n, L), lambda i: (0, 0))
    return pl.pallas_call(
        _scan_kernel,
        grid=(S // tc,),
        in_specs=[blk] * 5 + [cst] * 5,
        out_specs=blk,
        out_shape=jax.ShapeDtypeStruct((S, n, L), F32),
        scratch_shapes=[pltpu.VMEM((n, n, L), F32)] + [pltpu.VMEM((tc, n, L), F32)] * 7,
        compiler_params=_cparams(("arbitrary",)),
        name="wkv_scan",
    )(r, k, v, w, a, kkc, kac, rkc, lg, lb)


LOG2E = 1.4426950408889634
N_PIECES = 3


def _aux_base(h):
    return (1 - h % 2) * HEAD_DIM


def _fox_prep_kernel(q_ref, k_ref, v_ref, f_ref, fb_ref, tri_ref, selq_ref, selk_ref, cq_ref, ck_ref,
                     qa_ref, ka_ref, vt_ref, carry, *, nh):
    @pl.when(pl.program_id(1) == 0)
    def _():
        carry[...] = jnp.zeros_like(carry)

    z = f_ref[...] + fb_ref[...]
    lf = -_softplus(-z)
    inc = jnp.dot(tri_ref[...], lf, precision=HIGHEST, preferred_element_type=F32) + carry[...]
    ts = inc.shape[0]
    carry[...] = inc[ts - 1:ts, :]

    rest = inc * LOG2E
    pieces = []
    for _ in range(N_PIECES):
        piece = rest.astype(BF16)
        pieces.append(piece)
        rest = rest - piece.astype(F32)
    pcs = jnp.concatenate(pieces, axis=1)
    aux_q = jnp.dot(pcs, selq_ref[...], preferred_element_type=F32) + cq_ref[...]
    aux_k = jnp.dot(pcs, selk_ref[...], preferred_element_type=F32) + ck_ref[...]

    lane = lax.broadcasted_iota(I32, (1, nh * LANES), 1)
    own = ((lane // HEAD_DIM) % 2) == ((lane // LANES) % 2)
    q = q_ref[...]
    k = k_ref[...]
    dup = lambda t: jnp.concatenate(
        [t[:, (h // 2) * LANES:(h // 2 + 1) * LANES] for h in range(nh)], axis=1)
    qa_ref[...] = jnp.where(own, dup(q), aux_q.astype(BF16))
    ka_ref[...] = jnp.where(own, dup(k), aux_k.astype(BF16))

    vt = v_ref[...].astype(F32).T
    row = lax.broadcasted_iota(I32, (LANES, 1), 0)
    groups = []
    for h in range(nh):
        pair_rows = vt[(h // 2) * LANES:(h // 2 + 1) * LANES]
        own_rows = (row // HEAD_DIM) == (h % 2)
        ones_row = (row == _aux_base(h)).astype(F32)
        groups.append(jnp.where(own_rows, pair_rows, ones_row))
    vt_ref[0] = jnp.concatenate(groups, axis=0).astype(BF16)


def _fox_prep(q, k, v, f, fb_pad, B, S, nh, ts):
    T, C = q.shape
    nt = S // ts
    G = nh * LANES
    tri = (lax.broadcasted_iota(I32, (ts, ts), 0) >= lax.broadcasted_iota(I32, (ts, ts), 1)).astype(F32)
    selq = jnp.zeros((N_PIECES * LANES, G), F32)
    selk = jnp.zeros((N_PIECES * LANES, G), F32)
    cq = jnp.zeros((1, G), F32)
    ck = jnp.zeros((1, G), F32)
    for h in range(nh):
        base = h * LANES + _aux_base(h)
        for pc in range(N_PIECES):
            selq = selq.at[pc * LANES + h, base + pc].set(1.0)
            selk = selk.at[pc * LANES + h, base + N_PIECES + pc].set(-1.0)
            ck = ck.at[0, base + pc].set(1.0)
            cq = cq.at[0, base + N_PIECES + pc].set(1.0)
    row = lambda b, i: (b * nt + i, 0)
    const = lambda b, i: (0, 0)
    return pl.pallas_call(
        functools.partial(_fox_prep_kernel, nh=nh),
        grid=(B, nt),
        in_specs=[pl.BlockSpec((ts, C), row), pl.BlockSpec((ts, C), row), pl.BlockSpec((ts, C), row),
                  pl.BlockSpec((ts, LANES), row), pl.BlockSpec((1, LANES), const),
                  pl.BlockSpec((ts, ts), const),
                  pl.BlockSpec((N_PIECES * LANES, G), const), pl.BlockSpec((N_PIECES * LANES, G), const),
                  pl.BlockSpec((1, G), const), pl.BlockSpec((1, G), const)],
        out_specs=[pl.BlockSpec((ts, G), row), pl.BlockSpec((ts, G), row),
                   pl.BlockSpec((1, G, ts), lambda b, i: (b, 0, i))],
        out_shape=[jax.ShapeDtypeStruct((T, G), BF16), jax.ShapeDtypeStruct((T, G), BF16),
                   jax.ShapeDtypeStruct((B, G, S), BF16)],
        scratch_shapes=[pltpu.VMEM((1, LANES), F32)],
        compiler_params=_cparams(("arbitrary", "arbitrary")),
        name="fox_prep",
    )(q, k, v, f, fb_pad, tri, selq.astype(BF16), selk.astype(BF16), cq, ck)


def _fox_kernel(q_ref, k_ref, vt_ref, o_ref, acc, mrow, s_a, s_b, lim, *, tq):
    qi = pl.program_id(2)
    tk = tq // 2
    n = 2 * (qi + 1)
    acc[...] = jnp.zeros_like(acc)
    mrow[...] = jnp.full_like(mrow, MASK_VALUE)

    @pl.when((pl.program_id(0) == 0) & (pl.program_id(1) == 0) & (qi == 0))
    def _():
        key = lax.broadcasted_iota(I32, (tk, tq), 0)
        query = lax.broadcasted_iota(I32, (tk, tq), 1)
        lim[0] = jnp.full((tk, tq), -MASK_VALUE, F32)
        lim[1] = jnp.where(key <= query, -MASK_VALUE, MASK_VALUE)
        lim[2] = jnp.where(key + tk <= query, -MASK_VALUE, MASK_VALUE)

    def scores(kb, dst):
        off = pl.multiple_of(jnp.minimum(kb, n - 1) * tk, tk)
        for hh in range(2):
            grp = slice(hh * LANES, (hh + 1) * LANES)
            dst[hh] = lax.dot_general(k_ref[0, pl.ds(off, tk), grp], q_ref[0, :, grp],
                                      (((1,), (1,)), ((), ())), preferred_element_type=F32)

    def softmax_pv(kb, src):
        off = pl.multiple_of(kb * tk, tk)
        which = jnp.maximum(kb - (n - 3), 0)
        for hh in range(2):
            grp = slice(hh * LANES, (hh + 1) * LANES)
            st = jnp.minimum(src[hh], lim[which])
            m_old = mrow[hh]
            m_new = jnp.maximum(m_old, jnp.max(st, axis=0, keepdims=True))
            mrow[hh] = m_new
            p = jnp.exp2(st - m_new).astype(BF16)
            pv = jnp.dot(vt_ref[0, grp, pl.ds(off, tk)], p, preferred_element_type=F32)
            acc[hh] = jnp.exp2(m_old - m_new) * acc[hh] + pv

    scores(0, s_a)

    def trip(j, c):
        scores(2 * j + 1, s_b)
        softmax_pv(2 * j, s_a)
        scores(2 * j + 2, s_a)
        softmax_pv(2 * j + 1, s_b)
        return c

    lax.fori_loop(0, qi + 1, trip, 0)

    row = lax.broadcasted_iota(I32, (LANES, 1), 0)
    out_t = jnp.zeros((LANES, tq), F32)
    for hh in range(2):
        a = acc[hh]
        base = _aux_base(hh)
        l = a[base:base + 1, :]
        out_t = jnp.where((row // HEAD_DIM) == hh, a / l, out_t)
    o_ref[0] = out_t.T


def _fox_attention(qa, ka, vt, c_fox, tq):
    B, S, G = qa.shape
    nh = G // LANES
    return pl.pallas_call(
        functools.partial(_fox_kernel, tq=tq),
        grid=(B, nh // 2, S // tq),
        in_specs=[pl.BlockSpec((1, tq, 2 * LANES), lambda b, p, i: (b, i, p)),
                  pl.BlockSpec((1, S, 2 * LANES), lambda b, p, i: (b, 0, p)),
                  pl.BlockSpec((1, 2 * LANES, S), lambda b, p, i: (b, p, 0))],
        out_specs=pl.BlockSpec((1, tq, LANES), lambda b, p, i: (b, i, p)),
        out_shape=jax.ShapeDtypeStruct((B, S, c_fox), F32),
        scratch_shapes=[pltpu.VMEM((2, LANES, tq), F32), pltpu.VMEM((2, 1, tq), F32),
                        pltpu.VMEM((2, tq // 2, tq), F32), pltpu.VMEM((2, tq // 2, tq), F32),
                        pltpu.VMEM((3, tq // 2, tq), F32)],
        compiler_params=_cparams(("arbitrary", "arbitrary", "arbitrary")),
        name="fox_attention",
    )(qa, ka, vt)


def _mix_kernel(yr_ref, g_ref, yf_ref, x_ref, mod_ref, wo_ref, n2_ref, wr_ref,
                x1_ref, h2_ref, lg_ref, *, c):
    m = mod_ref[0]
    a = (yr_ref[...] * g_ref[...]).astype(BF16)
    b = yf_ref[...].astype(BF16)
    mix = (jnp.dot(a, wo_ref[0:c, :], preferred_element_type=F32)
           + jnp.dot(b, wo_ref[c:, :], preferred_element_type=F32))
    x1 = x_ref[...] + m[2:3] * mix
    x1_ref[...] = x1
    h2 = _rms(x1) * n2_ref[...] * (1.0 + m[4:5]) + m[3:4]
    h2_ref[...] = _pack_bf16_halves(h2)
    lg_ref[...] = _dot_split(h2, wr_ref)


def _pack_bf16_halves(x):
    w = x.shape[1] // 2
    bits = lax.bitcast_convert_type(x.astype(BF16).astype(F32), jnp.uint32)
    return (bits[:, :w] >> 16) | (bits[:, w:] & jnp.uint32(0xFFFF0000))


def _unpack_bf16_halves(words):
    lo = lax.bitcast_convert_type(words << 16, F32)
    hi = lax.bitcast_convert_type(words & jnp.uint32(0xFFFF0000), F32)
    return jnp.concatenate([lo, hi], axis=1).astype(BF16)


def _mix(y_rwkv, g, y_fox, x2, mod3, w_out, norm2_g, w_router, S, tm):
    T, D = x2.shape
    c = g.shape[1]
    nt = S // tm
    row = lambda i: (i, 0)
    const = lambda i: (0, 0)
    return pl.pallas_call(
        functools.partial(_mix_kernel, c=c),
        grid=(T // tm,),
        in_specs=[pl.BlockSpec((tm, c), lambda i: (i % nt, i // nt)),
                  pl.BlockSpec((tm, c), row), pl.BlockSpec((tm, c), row),
                  pl.BlockSpec((tm, D), row),
                  pl.BlockSpec((1, 6, D), lambda i: ((i * tm) // S, 0, 0)),
                  pl.BlockSpec((D, D), const), pl.BlockSpec((1, D), const),
                  pl.BlockSpec((2, D, LANES), lambda i: (0, 0, 0))],
        out_specs=[pl.BlockSpec((tm, D), row), pl.BlockSpec((tm, D // 2), row), pl.BlockSpec((tm, LANES), row)],
        out_shape=[jax.ShapeDtypeStruct((T, D), F32), jax.ShapeDtypeStruct((T, D // 2), jnp.uint32),
                   jax.ShapeDtypeStruct((T, LANES), F32)],
        compiler_params=_cparams(("arbitrary",)),
        name="out_proj_norm2_router",
    )(y_rwkv, g, y_fox, x2, mod3, w_out, norm2_g, w_router)


E_ROW0 = 8


def _first_argmax(vals, n):
    mx = jnp.max(vals, axis=0, keepdims=True)
    idx = lax.broadcasted_iota(I32, vals.shape, 0).astype(F32)
    first = jnp.min(jnp.where(vals == mx, idx, float(n)), axis=0, keepdims=True)
    return first.astype(I32), mx


def _route_kernel(lg_ref, bias_ref, tri_ref, ids_ref, wtok_ref, cnt_ref, carry):
    @pl.when(pl.program_id(0) == 0)
    def _():
        carry[...] = jnp.zeros_like(carry)

    lt = (lg_ref[...] + bias_ref[...]).T
    tm = lt.shape[1]
    grp = lt[0:N_GROUPS]
    ge = jnp.exp(grp - jnp.max(grp, axis=0, keepdims=True))
    gp = ge / jnp.sum(ge, axis=0, keepdims=True)
    g_sel, p_g = _first_argmax(gp, N_GROUPS)
    sel = jnp.zeros((EXPERTS_PER_GROUP, tm), F32)
    for g in range(N_GROUPS):
        lo = E_ROW0 + g * EXPERTS_PER_GROUP
        sel = jnp.where(g_sel == g, lt[lo:lo + EXPERTS_PER_GROUP], sel)
    ee = jnp.exp(sel - jnp.max(sel, axis=0, keepdims=True))
    ep = ee / jnp.sum(ee, axis=0, keepdims=True)
    i0, p0 = _first_argmax(ep, EXPERTS_PER_GROUP)
    idx8 = lax.broadcasted_iota(I32, ep.shape, 0)
    i1, p1 = _first_argmax(jnp.where(idx8 == i0, -1.0, ep), EXPERTS_PER_GROUP)
    den = p0 + p1
    w0 = p_g * p0 / den
    w1 = p_g * p1 / den
    e0 = g_sel * EXPERTS_PER_GROUP + i0
    e1 = g_sel * EXPERTS_PER_GROUP + i1

    ide = lax.broadcasted_iota(I32, (N_EXPERTS, tm), 0)
    oh0 = ide == e0
    oh1 = ide == e1
    oh = oh0.astype(F32) + oh1.astype(F32)
    incl = jnp.dot(oh.astype(BF16), tri_ref[...], preferred_element_type=F32)
    base = carry[...] + (incl - oh)
    r0 = jnp.sum(jnp.where(oh0, base, 0.0), axis=0, keepdims=True)
    r1 = jnp.sum(jnp.where(oh1, base, 0.0), axis=0, keepdims=True)
    carry[...] = carry[...] + incl[:, tm - 1:tm]
    cnt_ref[...] = jnp.broadcast_to(carry[...], cnt_ref.shape)
    ids_ref[...] = jnp.concatenate(
        [e0, e1, r0.astype(I32), r1.astype(I32), jnp.zeros((4, tm), I32)], axis=0)
    wtok_ref[...] = jnp.concatenate([w0, w1, jnp.zeros((LANES - 2, tm), F32)], axis=0).T


def _route(logits, bias_row, tm):
    T = logits.shape[0]
    tri = (lax.broadcasted_iota(I32, (tm, tm), 0) <= lax.broadcasted_iota(I32, (tm, tm), 1)).astype(BF16)
    return pl.pallas_call(
        _route_kernel,
        grid=(T // tm,),
        in_specs=[pl.BlockSpec((tm, LANES), lambda i: (i, 0)),
                  pl.BlockSpec((1, LANES), lambda i: (0, 0)),
                  pl.BlockSpec((tm, tm), lambda i: (0, 0))],
        out_specs=[pl.BlockSpec((8, tm), lambda i: (0, i)),
                   pl.BlockSpec((tm, LANES), lambda i: (i, 0)),
                   pl.BlockSpec((N_EXPERTS, LANES), lambda i: (0, 0))],
        out_shape=[jax.ShapeDtypeStruct((8, T), I32), jax.ShapeDtypeStruct((T, LANES), F32),
                   jax.ShapeDtypeStruct((N_EXPERTS, LANES), F32)],
        scratch_shapes=[pltpu.VMEM((N_EXPERTS, 1), F32)],
        compiler_params=_cparams(("arbitrary",)),
        name="route_rank",
    )(logits, bias_row, tri)


def _dest_kernel(ids_ref, ps_ref, d_ref):
    ids = ids_ref[...]
    tm = ids.shape[1]
    ide = lax.broadcasted_iota(I32, (N_EXPERTS, tm), 0)
    ps = ps_ref[...]
    rows = [jnp.sum(jnp.where(ide == ids[k:k + 1], ps, 0), axis=0, keepdims=True) + ids[2 + k:3 + k]
            for k in range(2)]
    d_ref[...] = jnp.concatenate(rows + [jnp.zeros((6, tm), I32)], axis=0)


def _dest_rows(ids, pstarts, tm):
    T = ids.shape[1]
    return pl.pallas_call(
        _dest_kernel,
        grid=(T // tm,),
        in_specs=[pl.BlockSpec((8, tm), lambda i: (0, i)), pl.BlockSpec((N_EXPERTS, 1), lambda i: (0, 0))],
        out_specs=pl.BlockSpec((8, tm), lambda i: (0, i)),
        out_shape=jax.ShapeDtypeStruct((8, T), I32),
        compiler_params=_cparams(("arbitrary",)),
        name="dest_rows",
    )(ids, pstarts.reshape(N_EXPERTS, 1))


def _row_copy(src, dst, sem):
    return pltpu.make_async_copy(src, dst, sem)


def _dispatch_kernel(ps_ref, pe_ref, nu_ref, d0_ref, d1_ref, h_ref, xs_out, zbuf, sem, zsem, *, n_blocks, n_tail):
    tmd = h_ref.shape[0]
    er = zbuf.shape[0]

    @pl.when(pl.program_id(0) == 0)
    def _():
        zbuf[...] = jnp.zeros_like(zbuf)

        def pad_copy(e):
            return _row_copy(zbuf, xs_out.at[pl.ds(pl.multiple_of(pe_ref[e] - er, er), er)], zsem)

        def tail_copy(j):
            return _row_copy(zbuf, xs_out.at[pl.ds(pl.multiple_of((nu_ref[0] + j) * er, er), er)], zsem)

        def each(fn):
            def pads(e, c):
                @pl.when(pe_ref[e] > ps_ref[e])
                def _():
                    fn(pad_copy(e))
                return c

            def tails(j, c):
                @pl.when(nu_ref[0] + j < n_blocks)
                def _():
                    fn(tail_copy(j))
                return c

            lax.fori_loop(0, N_EXPERTS, pads, 0)
            lax.fori_loop(0, n_tail, tails, 0)

        each(lambda cp: cp.start())
        each(lambda cp: cp.wait())

    def issue(j, c):
        _row_copy(h_ref.at[pl.ds(j, 1)], xs_out.at[pl.ds(d0_ref[j], 1)], sem).start()
        _row_copy(h_ref.at[pl.ds(j, 1)], xs_out.at[pl.ds(d1_ref[j], 1)], sem).start()
        return c

    lax.fori_loop(0, tmd, issue, 0, unroll=ISSUE_UNROLL)

    def drain(j, c):
        _row_copy(h_ref.at[pl.ds(0, 1)], xs_out.at[pl.ds(0, 1)], sem).wait()
        _row_copy(h_ref.at[pl.ds(0, 1)], xs_out.at[pl.ds(0, 1)], sem).wait()
        return c

    lax.fori_loop(0, tmd, drain, 0)


ISSUE_UNROLL = 4


def _dispatch(pstarts, pends, n_used, d0, d1, h2_words, n_rows, tmd):
    T, W = h2_words.shape
    n_blocks = n_rows // EXPERT_ROWS
    n_tail = n_blocks - (2 * T) // EXPERT_ROWS
    grid_spec = pltpu.PrefetchScalarGridSpec(
        num_scalar_prefetch=3,
        grid=(T // tmd,),
        in_specs=[pl.BlockSpec((tmd,), lambda i, *_: (i,), memory_space=pltpu.SMEM),
                  pl.BlockSpec((tmd,), lambda i, *_: (i,), memory_space=pltpu.SMEM),
                  pl.BlockSpec((tmd, W), lambda i, *_: (i, 0))],
        out_specs=pl.BlockSpec(memory_space=pl.ANY),
        scratch_shapes=[pltpu.VMEM((EXPERT_ROWS, W), h2_words.dtype),
                        pltpu.SemaphoreType.DMA(()), pltpu.SemaphoreType.DMA(())],
    )
    return pl.pallas_call(
        functools.partial(_dispatch_kernel, n_blocks=n_blocks, n_tail=n_tail),
        grid_spec=grid_spec,
        out_shape=jax.ShapeDtypeStruct((n_rows, W), h2_words.dtype),
        compiler_params=_cparams(("arbitrary",)),
        name="moe_dispatch",
    )(pstarts, pends, n_used, d0, d1, h2_words)


def _expert_kernel(be_ref, nu_ref, x_ref, wg_ref, wu_ref, wd_ref, y_ref, wg_s, wu_s, wd_s):
    i = pl.program_id(0)
    used = i < nu_ref[0]

    @pl.when(used & ((i == 0) | (be_ref[i] != be_ref[jnp.maximum(i - 1, 0)])))
    def _():
        wg_s[...] = wg_ref[0].astype(BF16)
        wu_s[...] = wu_ref[0].astype(BF16)
        wd_s[...] = wd_ref[0].astype(BF16)

    @pl.when(used)
    def _():
        x = _unpack_bf16_halves(x_ref[...])
        g = jnp.dot(x, wg_s[...], preferred_element_type=F32)
        u = jnp.dot(x, wu_s[...], preferred_element_type=F32)
        hid = (g * _sigmoid(g) * u).astype(BF16)
        y_ref[...] = jnp.dot(hid, wd_s[...], preferred_element_type=F32)

    @pl.when(jnp.logical_not(used))
    def _():
        y_ref[...] = jnp.zeros_like(y_ref)


def _experts(blk_e, n_used, xs_words, wg, wu, wd, tme):
    P, W = xs_words.shape
    _, D, F = wg.shape
    grid_spec = pltpu.PrefetchScalarGridSpec(
        num_scalar_prefetch=2,
        grid=(P // tme,),
        in_specs=[pl.BlockSpec((tme, W), lambda i, be, nu: (i, 0)),
                  pl.BlockSpec((1, D, F), lambda i, be, nu: (be[i], 0, 0)),
                  pl.BlockSpec((1, D, F), lambda i, be, nu: (be[i], 0, 0)),
                  pl.BlockSpec((1, F, D), lambda i, be, nu: (be[i], 0, 0))],
        out_specs=pl.BlockSpec((tme, D), lambda i, be, nu: (i, 0)),
        scratch_shapes=[pltpu.VMEM((D, F), BF16), pltpu.VMEM((D, F), BF16), pltpu.VMEM((F, D), BF16)],
    )
    return pl.pallas_call(
        _expert_kernel,
        grid_spec=grid_spec,
        out_shape=jax.ShapeDtypeStruct((P, D), F32),
        compiler_params=_cparams(("arbitrary",)),
        name="moe_experts",
    )(blk_e, n_used, xs_words, wg, wu, wd)


def _combine_kernel(d0c_ref, d1c_ref, d0n_ref, d1n_ref, ys_ref, wtok_ref, x1_ref, mod_ref, gf_ref, o_ref,
                    ybuf, sems):
    tmc = x1_ref.shape[0]
    i = pl.program_id(0)
    slot = i % 2

    def gather(d0_ref, d1_ref, s):
        def issue(j, c):
            _row_copy(ys_ref.at[pl.ds(d0_ref[j], 1)], ybuf.at[s, 0, pl.ds(j, 1)], sems.at[s]).start()
            _row_copy(ys_ref.at[pl.ds(d1_ref[j], 1)], ybuf.at[s, 1, pl.ds(j, 1)], sems.at[s]).start()
            return c

        lax.fori_loop(0, tmc, issue, 0, unroll=ISSUE_UNROLL)

    @pl.when(i == 0)
    def _():
        gather(d0c_ref, d1c_ref, 0)

    @pl.when(i + 1 < pl.num_programs(0))
    def _():
        gather(d0n_ref, d1n_ref, 1 - slot)

    def drain(j, c):
        _row_copy(ys_ref.at[pl.ds(0, 1)], ybuf.at[slot, 0, pl.ds(0, 1)], sems.at[slot]).wait()
        _row_copy(ys_ref.at[pl.ds(0, 1)], ybuf.at[slot, 1, pl.ds(0, 1)], sems.at[slot]).wait()
        return c

    lax.fori_loop(0, tmc, drain, 0)

    m = mod_ref[0]
    w = wtok_ref[...]
    ff = w[:, 0:1] * ybuf[slot, 0] + w[:, 1:2] * ybuf[slot, 1]
    x2 = x1_ref[...] + m[5:6] * ff
    o_ref[...] = _rms(x2) * gf_ref[...]


def _combine(d0, d1, ys, wtok, x1, mod3, norm_f_g, S, tmc):
    T, D = x1.shape
    n = T // tmc
    cur = lambda i: (i,)
    nxt = lambda i: (jnp.minimum(i + 1, n - 1),)
    smem = lambda f: pl.BlockSpec((tmc,), f, memory_space=pltpu.SMEM)
    return pl.pallas_call(
        _combine_kernel,
        grid=(n,),
        in_specs=[smem(cur), smem(cur), smem(nxt), smem(nxt),
                  pl.BlockSpec(memory_space=pl.ANY),
                  pl.BlockSpec((tmc, LANES), lambda i: (i, 0)),
                  pl.BlockSpec((tmc, D), lambda i: (i, 0)),
                  pl.BlockSpec((1, 6, D), lambda i: ((i * tmc) // S, 0, 0)),
                  pl.BlockSpec((1, D), lambda i: (0, 0))],
        out_specs=pl.BlockSpec((tmc, D), lambda i: (i, 0)),
        out_shape=jax.ShapeDtypeStruct((T, D), F32),
        scratch_shapes=[pltpu.VMEM((2, 2, tmc, D), F32), pltpu.SemaphoreType.DMA((2,))],
        compiler_params=_cparams(("arbitrary",)),
        name="moe_combine_final_norm",
    )(d0, d1, d0, d1, ys, wtok, x1, mod3, norm_f_g)


def _pick(n, pref):
    t = min(pref, n)
    while n % t:
        t //= 2
    return t


def _layer(x, mod3, norm1_g, w_in, rwkv_mu, rwkv_w0, rwkv_w_up, rwkv_a0, rwkv_a_up, rwkv_g_up, rwkv_k_k,
           rwkv_k_a, rwkv_r_k, rwkv_lnx_g, rwkv_lnx_b, fox_f_bias, w_out, norm2_g, moe_w_grp, moe_b_grp,
           moe_w_rt, moe_b_rt, moe_w_gate, moe_w_up, moe_w_down):
    B, S, D = x.shape
    T = B * S
    c = rwkv_w0.shape[0]
    nh_r = c // HEAD_DIM
    n_rwkv = rwkv_mu.shape[0]
    nh_f = fox_f_bias.shape[0]
    c_fox = nh_f * HEAD_DIM
    d_lora = rwkv_w_up.shape[0]
    x2 = x.reshape(T, D)

    w_pad = jnp.pad(w_in, ((0, 0), (0, LANES - nh_f))).astype(BF16)
    p_rwkv, q, k, v, f = _project(x2, mod3, norm1_g.reshape(1, D), w_pad, S, n_rwkv, c_fox, _pick(T, 256))

    wup_pad = jnp.pad(rwkv_w_up, ((0, LANES - d_lora), (0, 0)))
    aup_pad = jnp.pad(rwkv_a_up, ((d_lora, LANES - d_lora - rwkv_a_up.shape[0]), (0, 0)))
    r_t, k_t, v_t, w_t, a_t, g_t = _rwkv_prep(
        p_rwkv, rwkv_mu.reshape(1, n_rwkv), rwkv_w0.reshape(1, c), _split_bf16(wup_pad), rwkv_a0.reshape(1, c),
        _split_bf16(aup_pad), _split_bf16(rwkv_g_up), B, S, c, _pick(S, 256))
    inst = B * nh_r
    to_scan = lambda t: t.reshape(S, inst, HEAD_DIM).transpose(0, 2, 1)
    per_inst = lambda p: jnp.tile(p.reshape(nh_r, HEAD_DIM).T, (1, B))
    y_scan = _wkv_scan(to_scan(r_t), to_scan(k_t), to_scan(v_t), to_scan(w_t), to_scan(a_t),
                       per_inst(rwkv_k_k), per_inst(rwkv_k_a), per_inst(rwkv_r_k), per_inst(rwkv_lnx_g),
                       per_inst(rwkv_lnx_b), _pick(S, 32))
    y_rwkv = y_scan.transpose(0, 2, 1).reshape(S, B * c)

    fb_pad = jnp.pad(fox_f_bias, (0, LANES - nh_f)).reshape(1, LANES)
    qa, ka, vt = _fox_prep(q, k, v, f, fb_pad, B, S, nh_f, _pick(S, 512))
    G = nh_f * LANES
    y_fox = _fox_attention(qa.reshape(B, S, G), ka.reshape(B, S, G), vt, c_fox, _pick(S, 512)).reshape(T, c_fox)

    w_router = jnp.zeros((D, LANES), F32)
    w_router = w_router.at[:, 0:N_GROUPS].set(moe_w_grp).at[:, E_ROW0:E_ROW0 + N_EXPERTS].set(moe_w_rt)
    b_router = jnp.zeros((1, LANES), F32)
    b_router = b_router.at[0, 0:N_GROUPS].set(moe_b_grp).at[0, E_ROW0:E_ROW0 + N_EXPERTS].set(moe_b_rt)
    x1, h2_words, logits = _mix(y_rwkv, g_t, y_fox, x2, mod3, w_out.astype(BF16), norm2_g.reshape(1, D),
                          _split_bf16(w_router), S, _pick(T, 256))

    ids, wtok, cnt = _route(logits, b_router, _pick(T, 1024))
    counts = cnt[:, 0].astype(I32)
    padded = (counts + EXPERT_ROWS - 1) // EXPERT_ROWS * EXPERT_ROWS
    pends = jnp.cumsum(padded).astype(I32)
    pstarts = pends - padded
    n_rows = (2 * T + N_EXPERTS * (EXPERT_ROWS - 1) + EXPERT_ROWS - 1) // EXPERT_ROWS * EXPERT_ROWS
    n_blocks = n_rows // EXPERT_ROWS
    blk_start = jnp.arange(n_blocks, dtype=I32) * EXPERT_ROWS
    blk_e = jnp.sum((pends[None, :] <= blk_start[:, None]).astype(I32), axis=1)
    blk_e = jnp.minimum(blk_e, N_EXPERTS - 1)
    n_used = pends[-1:] // EXPERT_ROWS

    dest = _dest_rows(ids, pstarts, _pick(T, 1024))
    d0, d1 = dest[0], dest[1]
    xs_words = _dispatch(pstarts, pends, n_used, d0, d1, h2_words, n_rows, _pick(T, 256))
    ys = _experts(blk_e, n_used, xs_words, moe_w_gate, moe_w_up, moe_w_down, EXPERT_ROWS)
    return x1, d0, d1, ys, wtok


def kernel(x, c, w_ada, b_ada, norm1_g, w_in, rwkv_mu, rwkv_w0, rwkv_w_up, rwkv_a0, rwkv_a_up, rwkv_g_up, rwkv_k_k, rwkv_k_a, rwkv_r_k, rwkv_lnx_g, rwkv_lnx_b, fox_f_bias, w_out, norm2_g, moe_w_grp, moe_b_grp, moe_w_rt, moe_b_rt, moe_w_gate, moe_w_up, moe_w_down, norm_f_g):
    B, S, D = x.shape
    assert w_ada.shape[0] == 1, "single-layer model"
    mod3 = _adaln_mod(c, w_ada[0], b_ada[0]).reshape(B, 6, D)
    x1, d0, d1, ys, wtok = _layer(
        x, mod3, norm1_g[0], w_in[0], rwkv_mu[0], rwkv_w0[0], rwkv_w_up[0], rwkv_a0[0], rwkv_a_up[0],
        rwkv_g_up[0], rwkv_k_k[0], rwkv_k_a[0], rwkv_r_k[0], rwkv_lnx_g[0], rwkv_lnx_b[0], fox_f_bias[0],
        w_out[0], norm2_g[0], moe_w_grp[0], moe_b_grp[0], moe_w_rt[0], moe_b_rt[0], moe_w_gate[0],
        moe_w_up[0], moe_w_down[0])
    out = _combine(d0, d1, ys, wtok, x1, mod3, norm_f_g.reshape(1, D), S, _pick(B * S, 256))
    return out.reshape(B, S, D)
```

```python
import functools

import jax
import jax.numpy as jnp
from jax import lax
from jax.experimental import pallas as pl
from jax.experimental.pallas import tpu as pltpu

F32 = jnp.float32
BF16 = jnp.bfloat16
I32 = jnp.int32
HIGHEST = lax.Precision.HIGHEST

HEAD_DIM = 64
N_GROUPS = 4
EXPERTS_PER_GROUP = 8
N_EXPERTS = N_GROUPS * EXPERTS_PER_GROUP
NORM_EPS = 1e-6
GN_EPS = 64e-5
LANES = 128
VMEM_LIMIT = 56 * 1024 * 1024

EXPERT_ROWS = 256
MASK_VALUE = -1e30


def _cparams(sem):
    return pltpu.CompilerParams(dimension_semantics=sem, vmem_limit_bytes=VMEM_LIMIT)


def _sigmoid(x):
    return 1.0 / (1.0 + jnp.exp(-x))


def _softplus(x):
    return jnp.maximum(x, 0.0) + jnp.log(1.0 + jnp.exp(-jnp.abs(x)))


def _split_bf16(w):
    hi = w.astype(BF16)
    return jnp.stack([hi, (w - hi.astype(F32)).astype(BF16)])


def _dot_split(a, w_ref):
    a_hi = a.astype(BF16)
    a_lo = (a - a_hi.astype(F32)).astype(BF16)
    w_hi = w_ref[0]
    return (jnp.dot(a_hi, w_hi, preferred_element_type=F32) + jnp.dot(a_lo, w_hi, preferred_element_type=F32)
            + jnp.dot(a_hi, w_ref[1], preferred_element_type=F32))


def _rms(x):
    return x * lax.rsqrt(jnp.mean(x * x, axis=-1, keepdims=True) + NORM_EPS)


def _mod_kernel(c_ref, w_ref, b_ref, o_ref):
    c = c_ref[...]
    cond = c * _sigmoid(c)
    o_ref[...] = jnp.dot(cond, w_ref[...], precision=HIGHEST, preferred_element_type=F32) + b_ref[...]


def _adaln_mod(c, w_ada, b_ada):
    B, D = c.shape
    N = w_ada.shape[1]
    tn = D
    return pl.pallas_call(
        _mod_kernel,
        grid=(N // tn,),
        in_specs=[pl.BlockSpec((B, D), lambda j: (0, 0)),
                  pl.BlockSpec((D, tn), lambda j: (0, j)),
                  pl.BlockSpec((1, tn), lambda j: (0, j))],
        out_specs=pl.BlockSpec((B, tn), lambda j: (0, j)),
        out_shape=jax.ShapeDtypeStruct((B, N), F32),
        compiler_params=_cparams(("arbitrary",)),
        name="adaln_mod",
    )(c, w_ada, b_ada.reshape(1, N))


def _proj_kernel(x_ref, mod_ref, g_ref, w_ref, pr_ref, q_ref, k_ref, v_ref, f_ref, *, n_rwkv, c_fox):
    m = mod_ref[0]
    h = (_rms(x_ref[...]) * g_ref[...] * (1.0 + m[1:2]) + m[0:1]).astype(BF16)
    o = n_rwkv
    pr_ref[...] = jnp.dot(h, w_ref[:, 0:o], preferred_element_type=F32)
    q = jnp.dot(h, w_ref[:, o:o + c_fox], preferred_element_type=F32)
    q_ref[...] = (q * (LOG2E * HEAD_DIM ** -0.5)).astype(BF16)
    k_ref[...] = jnp.dot(h, w_ref[:, o + c_fox:o + 2 * c_fox], preferred_element_type=F32).astype(BF16)
    v_ref[...] = jnp.dot(h, w_ref[:, o + 2 * c_fox:o + 3 * c_fox], preferred_element_type=F32).astype(BF16)
    f_ref[...] = jnp.dot(h, w_ref[:, o + 3 * c_fox:o + 3 * c_fox + LANES], preferred_element_type=F32)


def _project(x2, mod3, norm_g, w_pad, S, n_rwkv, c_fox, tm):
    T, D = x2.shape
    NW = w_pad.shape[1]
    row = lambda i: (i, 0)
    return pl.pallas_call(
        functools.partial(_proj_kernel, n_rwkv=n_rwkv, c_fox=c_fox),
        grid=(T // tm,),
        in_specs=[pl.BlockSpec((tm, D), row),
                  pl.BlockSpec((1, 6, D), lambda i: ((i * tm) // S, 0, 0)),
                  pl.BlockSpec((1, D), lambda i: (0, 0)),
                  pl.BlockSpec((D, NW), lambda i: (0, 0))],
        out_specs=[pl.BlockSpec((tm, n_rwkv), row), pl.BlockSpec((tm, c_fox), row),
                   pl.BlockSpec((tm, c_fox), row), pl.BlockSpec((tm, c_fox), row),
                   pl.BlockSpec((tm, LANES), row)],
        out_shape=[jax.ShapeDtypeStruct((T, n_rwkv), F32), jax.ShapeDtypeStruct((T, c_fox), BF16),
                   jax.ShapeDtypeStruct((T, c_fox), BF16), jax.ShapeDtypeStruct((T, c_fox), BF16),
                   jax.ShapeDtypeStruct((T, LANES), F32)],
        compiler_params=_cparams(("arbitrary",)),
        name="in_proj",
    )(x2, mod3, norm_g, w_pad)


def _rwkv_prep_kernel(p_ref, mu_ref, w0_ref, wup_ref, a0_ref, aup_ref, gup_ref,
                      r_ref, k_ref, v_ref, w_ref, a_ref, g_ref, carry, *, c):
    @pl.when(pl.program_id(1) == 0)
    def _():
        carry[...] = jnp.zeros_like(carry)

    p = p_ref[...]
    tt = p.shape[0]
    row = lax.broadcasted_iota(I32, (tt, 1), 0)
    prev = jnp.where(row == 0, carry[...], pltpu.roll(p, 1, 0))
    carry[...] = p[tt - 1:tt, :]
    xs = p + (prev - p) * mu_ref[...]
    r_ref[...] = xs[:, 0:c]
    k_ref[...] = xs[:, c:2 * c]
    v_ref[...] = xs[:, 2 * c:3 * c]
    lo = xs[:, 3 * c:3 * c + LANES]
    wl = _dot_split(jnp.tanh(lo), wup_ref)
    w_log = -_softplus(-(w0_ref[...] + wl)) - 0.5
    w_ref[...] = jnp.exp(-jnp.exp(w_log))
    al = _dot_split(lo, aup_ref)
    a_ref[...] = _sigmoid(a0_ref[...] + al)
    gd = _sigmoid(xs[:, 3 * c + LANES:3 * c + 2 * LANES])
    g_ref[...] = _dot_split(gd, gup_ref)


def _rwkv_prep(p_rwkv, mu, w0, wup_pad, a0, aup_pad, gup, B, S, c, tt):
    T, NR = p_rwkv.shape
    nt = S // tt
    row = lambda b, i: (b * nt + i, 0)
    const = lambda b, i: (0, 0)
    const3 = lambda b, i: (0, 0, 0)
    out = jax.ShapeDtypeStruct((T, c), F32)
    return pl.pallas_call(
        functools.partial(_rwkv_prep_kernel, c=c),
        grid=(B, nt),
        in_specs=[pl.BlockSpec((tt, NR), row), pl.BlockSpec((1, NR), const),
                  pl.BlockSpec((1, c), const), pl.BlockSpec((2, LANES, c), const3),
                  pl.BlockSpec((1, c), const), pl.BlockSpec((2, LANES, c), const3),
                  pl.BlockSpec((2, LANES, c), const3)],
        out_specs=[pl.BlockSpec((tt, c), lambda b, i: (i, b))] * 5 + [pl.BlockSpec((tt, c), row)],
        out_shape=[jax.ShapeDtypeStruct((S, B * c), F32)] * 5 + [out],
        scratch_shapes=[pltpu.VMEM((1, NR), F32)],
        compiler_params=_cparams(("arbitrary", "arbitrary")),
        name="rwkv_prep",
    )(p_rwkv, mu, w0, wup_pad, a0, aup_pad, gup)


ROW_UNROLL = 8


def _scan_kernel(r_ref, k_ref, v_ref, w_ref, a_ref, kkc_ref, kac_ref, rkc_ref, lg_ref, lb_ref,
                 y_ref, st, kk_s, b_s, km_s, rr_s, wprev_s, winc_s, yraw):
    @pl.when(pl.program_id(0) == 0)
    def _():
        st[...] = jnp.zeros_like(st)

    tc, n, lanes = r_ref.shape
    k = k_ref[...]
    a = a_ref[...]
    r = r_ref[...]
    kkr = k * kkc_ref[...][None]
    nrm = jnp.sqrt(jnp.sum(kkr * kkr, axis=1, keepdims=True))
    kk = kkr / jnp.maximum(nrm, 1e-12)
    km = k * (1.0 + (a - 1.0) * kac_ref[...][None])
    y_ref[...] = jnp.sum(r * km * rkc_ref[...][None], axis=1, keepdims=True) * v_ref[...]

    def cumulate(t, wc):
        wprev_s[t] = wc
        wc = wc * w_ref[t]
        winc_s[t] = wc
        return wc

    w_chunk = lax.fori_loop(0, tc, cumulate, jnp.ones((n, lanes), F32))
    winc = winc_s[...]
    inv = 1.0 / winc
    kk_s[...] = -(wprev_s[...] * kk)
    b_s[...] = kk * a * inv
    km_s[...] = km * inv
    rr_s[...] = r * winc

    def step(t, carry):
        kk_t = kk_s[t]
        b_t = b_s[t]
        k_t = km_s[t]
        r_t = rr_s[t]

        def rows(j, c2):
            for u in range(ROW_UNROLL):
                vv = j * ROW_UNROLL + u
                s_old = st[vv]
                sa = jnp.sum(s_old * kk_t, axis=0, keepdims=True)
                s_new = s_old + sa * b_t + v_ref[t, pl.ds(vv, 1), :] * k_t
                st[vv] = s_new
                yraw[t, pl.ds(vv, 1), :] = jnp.sum(s_new * r_t, axis=0, keepdims=True)
            return c2

        lax.fori_loop(0, n // ROW_UNROLL, rows, 0, unroll=2)
        return carry

    lax.fori_loop(0, tc, step, 0)
    st[...] = st[...] * w_chunk[None]

    y = yraw[...]
    mean = jnp.mean(y, axis=1, keepdims=True)
    yc = y - mean
    var = jnp.mean(yc * yc, axis=1, keepdims=True)
    yn = yc * lax.rsqrt(var + GN_EPS) * lg_ref[...][None] + lb_ref[...][None]
    y_ref[...] = y_ref[...] + yn


def _wkv_scan(r, k, v, w, a, kkc, kac, rkc, lg, lb, tc):
    S, n, L = r.shape
    blk = pl.BlockSpec((tc, n, L), lambda i: (i, 0, 0))
    cst = pl.BlockSpec((n, L), lambda i: (0, 0))
    return pl.pallas_call(
        _scan_kernel,
        grid=(S // tc,),
        in_specs=[blk] * 5 + [cst] * 5,
        out_specs=blk,
        out_shape=jax.ShapeDtypeStruct((S, n, L), F32),
        scratch_shapes=[pltpu.VMEM((n, n, L), F32)] + [pltpu.VMEM((tc, n, L), F32)] * 7,
        compiler_params=_cparams(("arbitrary",)),
        name="wkv_scan",
    )(r, k, v, w, a, kkc, kac, rkc, lg, lb)


LOG2E = 1.4426950408889634
N_PIECES = 3


def _aux_base(h):
    return (1 - h % 2) * HEAD_DIM


def _fox_prep_kernel(q_ref, k_ref, v_ref, f_ref, fb_ref, tri_ref, sel_ref, mq_ref, cq_ref, ck_ref,
                     qa_ref, ka_ref, vt_ref, carry, *, nh):
    @pl.when(pl.program_id(1) == 0)
    def _():
        carry[...] = jnp.zeros_like(carry)

    lane1 = lax.broadcasted_iota(I32, (1, LANES), 1)

    def pack_pieces(x):
        pieces, rest = [], x
        for _ in range(N_PIECES):
            piece = rest.astype(BF16).astype(F32)
            pieces.append(piece)
            rest = rest - piece
        packed = jnp.where(lane1 < nh, pieces[0],
                           jnp.where(lane1 < 2 * nh, pltpu.roll(pieces[1], nh, 1),
                                     jnp.where(lane1 < 3 * nh, pltpu.roll(pieces[2], 2 * nh, 1), 0.0)))
        return packed.astype(BF16)

    z = f_ref[...] + fb_ref[...]
    lf = -_softplus(-z)
    part = jnp.dot(tri_ref[...], pack_pieces(lf), preferred_element_type=F32)
    inc = part + pltpu.roll(part, LANES - nh, 1) + pltpu.roll(part, LANES - 2 * nh, 1) + carry[...]
    ts = inc.shape[0]
    carry[...] = inc[ts - 1:ts, :]

    aux = jnp.dot(pack_pieces(inc * LOG2E), sel_ref[...], preferred_element_type=F32)
    aux_q = jnp.where(mq_ref[...] > 0, aux, cq_ref[...])
    aux_k = jnp.where(mq_ref[...] < 0, aux, ck_ref[...])

    lane = lax.broadcasted_iota(I32, (1, nh * LANES), 1)
    own = ((lane // HEAD_DIM) % 2) == ((lane // LANES) % 2)
    q = q_ref[...]
    k = k_ref[...]
    dup = lambda t: jnp.concatenate(
        [t[:, (h // 2) * LANES:(h // 2 + 1) * LANES] for h in range(nh)], axis=1)
    qa_ref[...] = jnp.where(own, dup(q), aux_q.astype(BF16))
    ka_ref[...] = jnp.where(own, dup(k), aux_k.astype(BF16))

    vt = v_ref[...].astype(F32).T
    row = lax.broadcasted_iota(I32, (LANES, 1), 0)
    groups = []
    for h in range(nh):
        pair_rows = vt[(h // 2) * LANES:(h // 2 + 1) * LANES]
        own_rows = (row // HEAD_DIM) == (h % 2)
        ones_row = (row == _aux_base(h)).astype(F32)
        groups.append(jnp.where(own_rows, pair_rows, ones_row))
    vt_ref[0] = jnp.concatenate(groups, axis=0).astype(BF16)


def _fox_prep(q, k, v, f, fb_pad, B, S, nh, ts):
    T, C = q.shape
    nt = S // ts
    G = nh * LANES
    tri = (lax.broadcasted_iota(I32, (ts, ts), 0) >= lax.broadcasted_iota(I32, (ts, ts), 1)).astype(BF16)
    sel = jnp.zeros((LANES, G), F32)
    mq = jnp.zeros((1, G), F32)
    cq = jnp.zeros((1, G), F32)
    ck = jnp.zeros((1, G), F32)
    for h in range(nh):
        base = h * LANES + _aux_base(h)
        for pc in range(N_PIECES):
            sel = sel.at[pc * nh + h, base + pc].set(1.0).at[pc * nh + h, base + N_PIECES + pc].set(-1.0)
            mq = mq.at[0, base + pc].set(1.0).at[0, base + N_PIECES + pc].set(-1.0)
            ck = ck.at[0, base + pc].set(1.0)
            cq = cq.at[0, base + N_PIECES + pc].set(1.0)
    row = lambda b, i: (b * nt + i, 0)
    const = lambda b, i: (0, 0)
    return pl.pallas_call(
        functools.partial(_fox_prep_kernel, nh=nh),
        grid=(B, nt),
        in_specs=[pl.BlockSpec((ts, C), row), pl.BlockSpec((ts, C), row), pl.BlockSpec((ts, C), row),
                  pl.BlockSpec((ts, LANES), row), pl.BlockSpec((1, LANES), const),
                  pl.BlockSpec((ts, ts), const),
                  pl.BlockSpec((LANES, G), const), pl.BlockSpec((1, G), const),
                  pl.BlockSpec((1, G), const), pl.BlockSpec((1, G), const)],
        out_specs=[pl.BlockSpec((ts, G), row), pl.BlockSpec((ts, G), row),
                   pl.BlockSpec((1, G, ts), lambda b, i: (b, 0, i))],
        out_shape=[jax.ShapeDtypeStruct((T, G), BF16), jax.ShapeDtypeStruct((T, G), BF16),
                   jax.ShapeDtypeStruct((B, G, S), BF16)],
        scratch_shapes=[pltpu.VMEM((1, LANES), F32)],
        compiler_params=_cparams(("arbitrary", "arbitrary")),
        name="fox_prep",
    )(q, k, v, f, fb_pad, tri, sel.astype(BF16), mq, cq, ck)


def _fox_kernel(q_ref, k_ref, vt_ref, o_ref, acc, mrow, s_a, s_b, lim, *, tq):
    qi = pl.program_id(2)
    tk = tq // 2
    n = 2 * (qi + 1)
    acc[...] = jnp.zeros_like(acc)
    mrow[...] = jnp.full_like(mrow, MASK_VALUE)

    @pl.when((pl.program_id(0) == 0) & (pl.program_id(1) == 0) & (qi == 0))
    def _():
        key = lax.broadcasted_iota(I32, (tk, tq), 0)
        query = lax.broadcasted_iota(I32, (tk, tq), 1)
        lim[0] = jnp.where(key <= query, -MASK_VALUE, MASK_VALUE)
        lim[1] = jnp.where(key + tk <= query, -MASK_VALUE, MASK_VALUE)

    def scores(kb, dst):
        off = pl.multiple_of(jnp.minimum(kb, n - 1) * tk, tk)
        for hh in range(2):
            grp = slice(hh * LANES, (hh + 1) * LANES)
            dst[hh] = lax.dot_general(k_ref[0, pl.ds(off, tk), grp], q_ref[0, :, grp],
                                      (((1,), (1,)), ((), ())), preferred_element_type=F32)

    def softmax_pv(kb, src, diag=None):
        off = pl.multiple_of(kb * tk, tk)
        for hh in range(2):
            grp = slice(hh * LANES, (hh + 1) * LANES)
            st = src[hh] if diag is None else jnp.minimum(src[hh], lim[diag])
            m_old = mrow[hh]
            m_new = jnp.maximum(m_old, jnp.max(st, axis=0, keepdims=True))
            mrow[hh] = m_new
            p = jnp.exp2(st - m_new).astype(BF16)
            pv = jnp.dot(vt_ref[0, grp, pl.ds(off, tk)], p, preferred_element_type=F32)
            acc[hh] = jnp.exp2(m_old - m_new) * acc[hh] + pv

    scores(0, s_a)

    def trip(j, c):
        scores(2 * j + 1, s_b)
        softmax_pv(2 * j, s_a)
        scores(2 * j + 2, s_a)
        softmax_pv(2 * j + 1, s_b)
        return c

    lax.fori_loop(0, qi, trip, 0)
    scores(n - 1, s_b)
    softmax_pv(n - 2, s_a, diag=0)
    softmax_pv(n - 1, s_b, diag=1)

    row = lax.broadcasted_iota(I32, (LANES, 1), 0)
    out_t = jnp.zeros((LANES, tq), F32)
    for hh in range(2):
        a = acc[hh]
        base = _aux_base(hh)
        l = a[base:base + 1, :]
        out_t = jnp.where((row // HEAD_DIM) == hh, a / l, out_t)
    o_ref[0] = out_t.T


def _fox_attention(qa, ka, vt, c_fox, tq):
    B, S, G = qa.shape
    nh = G // LANES
    return pl.pallas_call(
        functools.partial(_fox_kernel, tq=tq),
        grid=(B, nh // 2, S // tq),
        in_specs=[pl.BlockSpec((1, tq, 2 * LANES), lambda b, p, i: (b, i, p)),
                  pl.BlockSpec((1, S, 2 * LANES), lambda b, p, i: (b, 0, p)),
                  pl.BlockSpec((1, 2 * LANES, S), lambda b, p, i: (b, p, 0))],
        out_specs=pl.BlockSpec((1, tq, LANES), lambda b, p, i: (b, i, p)),
        out_shape=jax.ShapeDtypeStruct((B, S, c_fox), F32),
        scratch_shapes=[pltpu.VMEM((2, LANES, tq), F32), pltpu.VMEM((2, 1, tq), F32),
                        pltpu.VMEM((2, tq // 2, tq), F32), pltpu.VMEM((2, tq // 2, tq), F32),
                        pltpu.VMEM((2, tq // 2, tq), F32)],
        compiler_params=_cparams(("arbitrary", "arbitrary", "arbitrary")),
        name="fox_attention",
    )(qa, ka, vt)


def _mix_kernel(yr_ref, g_ref, yf_ref, x_ref, mod_ref, wo_ref, n2_ref, wr_ref,
                x1_ref, h2_ref, lg_ref, *, c):
    m = mod_ref[0]
    a = (yr_ref[...] * g_ref[...]).astype(BF16)
    b = yf_ref[...].astype(BF16)
    mix = (jnp.dot(a, wo_ref[0:c, :], preferred_element_type=F32)
           + jnp.dot(b, wo_ref[c:, :], preferred_element_type=F32))
    x1 = x_ref[...] + m[2:3] * mix
    x1_ref[...] = x1
    h2 = _rms(x1) * n2_ref[...] * (1.0 + m[4:5]) + m[3:4]
    h2_ref[...] = _pack_bf16_halves(h2)
    lg_ref[...] = _dot_split(h2, wr_ref)


def _pack_bf16_halves(x):
    w = x.shape[1] // 2
    bits = lax.bitcast_convert_type(x.astype(BF16).astype(F32), jnp.uint32)
    return (bits[:, :w] >> 16) | (bits[:, w:] & jnp.uint32(0xFFFF0000))


def _unpack_bf16_halves(words):
    lo = lax.bitcast_convert_type(words << 16, F32)
    hi = lax.bitcast_convert_type(words & jnp.uint32(0xFFFF0000), F32)
    return jnp.concatenate([lo, hi], axis=1).astype(BF16)


def _mix(y_rwkv, g, y_fox, x2, mod3, w_out, norm2_g, w_router, S, tm):
    T, D = x2.shape
    c = g.shape[1]
    nt = S // tm
    row = lambda i: (i, 0)
    const = lambda i: (0, 0)
    return pl.pallas_call(
        functools.partial(_mix_kernel, c=c),
        grid=(T // tm,),
        in_specs=[pl.BlockSpec((tm, c), lambda i: (i % nt, i // nt)),
                  pl.BlockSpec((tm, c), row), pl.BlockSpec((tm, c), row),
                  pl.BlockSpec((tm, D), row),
                  pl.BlockSpec((1, 6, D), lambda i: ((i * tm) // S, 0, 0)),
                  pl.BlockSpec((D, D), const), pl.BlockSpec((1, D), const),
                  pl.BlockSpec((2, D, LANES), lambda i: (0, 0, 0))],
        out_specs=[pl.BlockSpec((tm, D), row), pl.BlockSpec((tm, D // 2), row), pl.BlockSpec((tm, LANES), row)],
        out_shape=[jax.ShapeDtypeStruct((T, D), F32), jax.ShapeDtypeStruct((T, D // 2), jnp.uint32),
                   jax.ShapeDtypeStruct((T, LANES), F32)],
        compiler_params=_cparams(("arbitrary",)),
        name="out_proj_norm2_router",
    )(y_rwkv, g, y_fox, x2, mod3, w_out, norm2_g, w_router)


E_ROW0 = 8


def _first_argmax(vals, n):
    mx = jnp.max(vals, axis=0, keepdims=True)
    idx = lax.broadcasted_iota(I32, vals.shape, 0).astype(F32)
    first = jnp.min(jnp.where(vals == mx, idx, float(n)), axis=0, keepdims=True)
    return first.astype(I32), mx


def _route_kernel(lg_ref, bias_ref, tri_ref, ids_ref, wtok_ref, cnt_ref, carry):
    @pl.when(pl.program_id(0) == 0)
    def _():
        carry[...] = jnp.zeros_like(carry)

    lt = (lg_ref[...] + bias_ref[...]).T
    tm = lt.shape[1]
    grp = lt[0:N_GROUPS]
    ge = jnp.exp(grp - jnp.max(grp, axis=0, keepdims=True))
    gp = ge / jnp.sum(ge, axis=0, keepdims=True)
    g_sel, p_g = _first_argmax(gp, N_GROUPS)
    sel = jnp.zeros((EXPERTS_PER_GROUP, tm), F32)
    for g in range(N_GROUPS):
        lo = E_ROW0 + g * EXPERTS_PER_GROUP
        sel = jnp.where(g_sel == g, lt[lo:lo + EXPERTS_PER_GROUP], sel)
    ee = jnp.exp(sel - jnp.max(sel, axis=0, keepdims=True))
    ep = ee / jnp.sum(ee, axis=0, keepdims=True)
    i0, p0 = _first_argmax(ep, EXPERTS_PER_GROUP)
    idx8 = lax.broadcasted_iota(I32, ep.shape, 0)
    i1, p1 = _first_argmax(jnp.where(idx8 == i0, -1.0, ep), EXPERTS_PER_GROUP)
    den = p0 + p1
    w0 = p_g * p0 / den
    w1 = p_g * p1 / den
    e0 = g_sel * EXPERTS_PER_GROUP + i0
    e1 = g_sel * EXPERTS_PER_GROUP + i1

    ide = lax.broadcasted_iota(I32, (N_EXPERTS, tm), 0)
    oh0 = ide == e0
    oh1 = ide == e1
    oh = oh0.astype(F32) + oh1.astype(F32)
    incl = jnp.dot(oh.astype(BF16), tri_ref[...], preferred_element_type=F32)
    base = carry[...] + (incl - oh)
    r0 = jnp.sum(jnp.where(oh0, base, 0.0), axis=0, keepdims=True)
    r1 = jnp.sum(jnp.where(oh1, base, 0.0), axis=0, keepdims=True)
    carry[...] = carry[...] + incl[:, tm - 1:tm]
    cnt_ref[...] = jnp.broadcast_to(carry[...], cnt_ref.shape)
    ids_ref[...] = jnp.concatenate(
        [e0, e1, r0.astype(I32), r1.astype(I32), jnp.zeros((4, tm), I32)], axis=0)
    wtok_ref[...] = jnp.concatenate([w0, w1, jnp.zeros((LANES - 2, tm), F32)], axis=0).T


def _route(logits, bias_row, tm):
    T = logits.shape[0]
    tri = (lax.broadcasted_iota(I32, (tm, tm), 0) <= lax.broadcasted_iota(I32, (tm, tm), 1)).astype(BF16)
    return pl.pallas_call(
        _route_kernel,
        grid=(T // tm,),
        in_specs=[pl.BlockSpec((tm, LANES), lambda i: (i, 0)),
                  pl.BlockSpec((1, LANES), lambda i: (0, 0)),
                  pl.BlockSpec((tm, tm), lambda i: (0, 0))],
        out_specs=[pl.BlockSpec((8, tm), lambda i: (0, i)),
                   pl.BlockSpec((tm, LANES), lambda i: (i, 0)),
                   pl.BlockSpec((N_EXPERTS, LANES), lambda i: (0, 0))],
        out_shape=[jax.ShapeDtypeStruct((8, T), I32), jax.ShapeDtypeStruct((T, LANES), F32),
                   jax.ShapeDtypeStruct((N_EXPERTS, LANES), F32)],
        scratch_shapes=[pltpu.VMEM((N_EXPERTS, 1), F32)],
        compiler_params=_cparams(("arbitrary",)),
        name="route_rank",
    )(logits, bias_row, tri)


def _dest_kernel(ids_ref, ps_ref, d_ref):
    ids = ids_ref[...]
    tm = ids.shape[1]
    ide = lax.broadcasted_iota(I32, (N_EXPERTS, tm), 0)
    ps = ps_ref[...]
    rows = [jnp.sum(jnp.where(ide == ids[k:k + 1], ps, 0), axis=0, keepdims=True) + ids[2 + k:3 + k]
            for k in range(2)]
    d_ref[...] = jnp.concatenate(rows + [jnp.zeros((6, tm), I32)], axis=0)


def _dest_rows(ids, pstarts, tm):
    T = ids.shape[1]
    return pl.pallas_call(
        _dest_kernel,
        grid=(T // tm,),
        in_specs=[pl.BlockSpec((8, tm), lambda i: (0, i)), pl.BlockSpec((N_EXPERTS, 1), lambda i: (0, 0))],
        out_specs=pl.BlockSpec((8, tm), lambda i: (0, i)),
        out_shape=jax.ShapeDtypeStruct((8, T), I32),
        compiler_params=_cparams(("arbitrary",)),
        name="dest_rows",
    )(ids, pstarts.reshape(N_EXPERTS, 1))


def _row_copy(src, dst, sem):
    return pltpu.make_async_copy(src, dst, sem)


def _dispatch_kernel(ps_ref, pe_ref, nu_ref, d0_ref, d1_ref, h_ref, xs_out, zbuf, sem, zsem, *, n_blocks, n_tail):
    tmd = h_ref.shape[0]
    er = zbuf.shape[0]

    @pl.when(pl.program_id(0) == 0)
    def _():
        zbuf[...] = jnp.zeros_like(zbuf)

        def pad_copy(e):
            return _row_copy(zbuf, xs_out.at[pl.ds(pl.multiple_of(pe_ref[e] - er, er), er)], zsem)

        def tail_copy(j):
            return _row_copy(zbuf, xs_out.at[pl.ds(pl.multiple_of((nu_ref[0] + j) * er, er), er)], zsem)

        def each(fn):
            def pads(e, c):
                @pl.when(pe_ref[e] > ps_ref[e])
                def _():
                    fn(pad_copy(e))
                return c

            def tails(j, c):
                @pl.when(nu_ref[0] + j < n_blocks)
                def _():
                    fn(tail_copy(j))
                return c

            lax.fori_loop(0, N_EXPERTS, pads, 0)
            lax.fori_loop(0, n_tail, tails, 0)

        each(lambda cp: cp.start())
        each(lambda cp: cp.wait())

    def issue(j, c):
        _row_copy(h_ref.at[pl.ds(j, 1)], xs_out.at[pl.ds(d0_ref[j], 1)], sem).start()
        _row_copy(h_ref.at[pl.ds(j, 1)], xs_out.at[pl.ds(d1_ref[j], 1)], sem).start()
        return c

    lax.fori_loop(0, tmd, issue, 0, unroll=ISSUE_UNROLL)

    def drain(j, c):
        _row_copy(h_ref.at[pl.ds(0, 1)], xs_out.at[pl.ds(0, 1)], sem).wait()
        _row_copy(h_ref.at[pl.ds(0, 1)], xs_out.at[pl.ds(0, 1)], sem).wait()
        return c

    lax.fori_loop(0, tmd, drain, 0)


ISSUE_UNROLL = 4


def _dispatch(pstarts, pends, n_used, d0, d1, h2_words, n_rows, tmd):
    T, W = h2_words.shape
    n_blocks = n_rows // EXPERT_ROWS
    n_tail = n_blocks - (2 * T) // EXPERT_ROWS
    grid_spec = pltpu.PrefetchScalarGridSpec(
        num_scalar_prefetch=3,
        grid=(T // tmd,),
        in_specs=[pl.BlockSpec((tmd,), lambda i, *_: (i,), memory_space=pltpu.SMEM),
                  pl.BlockSpec((tmd,), lambda i, *_: (i,), memory_space=pltpu.SMEM),
                  pl.BlockSpec((tmd, W), lambda i, *_: (i, 0))],
        out_specs=pl.BlockSpec(memory_space=pl.ANY),
        scratch_shapes=[pltpu.VMEM((EXPERT_ROWS, W), h2_words.dtype),
                        pltpu.SemaphoreType.DMA(()), pltpu.SemaphoreType.DMA(())],
    )
    return pl.pallas_call(
        functools.partial(_dispatch_kernel, n_blocks=n_blocks, n_tail=n_tail),
        grid_spec=grid_spec,
        out_shape=jax.ShapeDtypeStruct((n_rows, W), h2_words.dtype),
        compiler_params=_cparams(("arbitrary",)),
        name="moe_dispatch",
    )(pstarts, pends, n_used, d0, d1, h2_words)


def _expert_kernel(be_ref, nu_ref, x_ref, wg_ref, wu_ref, wd_ref, y_ref, wg_s, wu_s, wd_s):
    i = pl.program_id(0)
    used = i < nu_ref[0]

    @pl.when(used & ((i == 0) | (be_ref[i] != be_ref[jnp.maximum(i - 1, 0)])))
    def _():
        wg_s[...] = wg_ref[0].astype(BF16)
        wu_s[...] = wu_ref[0].astype(BF16)
        wd_s[...] = wd_ref[0].astype(BF16)

    @pl.when(used)
    def _():
        x = _unpack_bf16_halves(x_ref[...])
        g = jnp.dot(x, wg_s[...], preferred_element_type=F32)
        u = jnp.dot(x, wu_s[...], preferred_element_type=F32)
        hid = (g * _sigmoid(g) * u).astype(BF16)
        y_ref[...] = jnp.dot(hid, wd_s[...], preferred_element_type=F32)

    @pl.when(jnp.logical_not(used))
    def _():
        y_ref[...] = jnp.zeros_like(y_ref)


def _experts(blk_e, n_used, xs_words, wg, wu, wd, tme):
    P, W = xs_words.shape
    _, D, F = wg.shape
    grid_spec = pltpu.PrefetchScalarGridSpec(
        num_scalar_prefetch=2,
        grid=(P // tme,),
        in_specs=[pl.BlockSpec((tme, W), lambda i, be, nu: (i, 0)),
                  pl.BlockSpec((1, D, F), lambda i, be, nu: (be[i], 0, 0)),
                  pl.BlockSpec((1, D, F), lambda i, be, nu: (be[i], 0, 0)),
                  pl.BlockSpec((1, F, D), lambda i, be, nu: (be[i], 0, 0))],
        out_specs=pl.BlockSpec((tme, D), lambda i, be, nu: (i, 0)),
        scratch_shapes=[pltpu.VMEM((D, F), BF16), pltpu.VMEM((D, F), BF16), pltpu.VMEM((F, D), BF16)],
    )
    return pl.pallas_call(
        _expert_kernel,
        grid_spec=grid_spec,
        out_shape=jax.ShapeDtypeStruct((P, D), F32),
        compiler_params=_cparams(("arbitrary",)),
        name="moe_experts",
    )(blk_e, n_used, xs_words, wg, wu, wd)


def _combine_kernel(d0c_ref, d1c_ref, d0n_ref, d1n_ref, ys_ref, wtok_ref, x1_ref, mod_ref, gf_ref, o_ref,
                    ybuf, sems):
    tmc = x1_ref.shape[0]
    i = pl.program_id(0)
    slot = i % 2

    def gather(d0_ref, d1_ref, s):
        def issue(j, c):
            _row_copy(ys_ref.at[pl.ds(d0_ref[j], 1)], ybuf.at[s, 0, pl.ds(j, 1)], sems.at[s]).start()
            _row_copy(ys_ref.at[pl.ds(d1_ref[j], 1)], ybuf.at[s, 1, pl.ds(j, 1)], sems.at[s]).start()
            return c

        lax.fori_loop(0, tmc, issue, 0, unroll=ISSUE_UNROLL)

    @pl.when(i == 0)
    def _():
        gather(d0c_ref, d1c_ref, 0)

    @pl.when(i + 1 < pl.num_programs(0))
    def _():
        gather(d0n_ref, d1n_ref, 1 - slot)

    def drain(j, c):
        _row_copy(ys_ref.at[pl.ds(0, 1)], ybuf.at[slot, 0, pl.ds(0, 1)], sems.at[slot]).wait()
        _row_copy(ys_ref.at[pl.ds(0, 1)], ybuf.at[slot, 1, pl.ds(0, 1)], sems.at[slot]).wait()
        return c

    lax.fori_loop(0, tmc, drain, 0)

    m = mod_ref[0]
    w = wtok_ref[...]
    ff = w[:, 0:1] * ybuf[slot, 0] + w[:, 1:2] * ybuf[slot, 1]
    x2 = x1_ref[...] + m[5:6] * ff
    o_ref[...] = _rms(x2) * gf_ref[...]


def _combine(d0, d1, ys, wtok, x1, mod3, norm_f_g, S, tmc):
    T, D = x1.shape
    n = T // tmc
    cur = lambda i: (i,)
    nxt = lambda i: (jnp.minimum(i + 1, n - 1),)
    smem = lambda f: pl.BlockSpec((tmc,), f, memory_space=pltpu.SMEM)
    return pl.pallas_call(
        _combine_kernel,
        grid=(n,),
        in_specs=[smem(cur), smem(cur), smem(nxt), smem(nxt),
                  pl.BlockSpec(memory_space=pl.ANY),
                  pl.BlockSpec((tmc, LANES), lambda i: (i, 0)),
                  pl.BlockSpec((tmc, D), lambda i: (i, 0)),
                  pl.BlockSpec((1, 6, D), lambda i: ((i * tmc) // S, 0, 0)),
                  pl.BlockSpec((1, D), lambda i: (0, 0))],
        out_specs=pl.BlockSpec((tmc, D), lambda i: (i, 0)),
        out_shape=jax.ShapeDtypeStruct((T, D), F32),
        scratch_shapes=[pltpu.VMEM((2, 2, tmc, D), F32), pltpu.SemaphoreType.DMA((2,))],
        compiler_params=_cparams(("arbitrary",)),
        name="moe_combine_final_norm",
    )(d0, d1, d0, d1, ys, wtok, x1, mod3, norm_f_g)


def _pick(n, pref):
    t = min(pref, n)
    while n % t:
        t //= 2
    return t


def _layer(x, mod3, norm1_g, w_in, rwkv_mu, rwkv_w0, rwkv_w_up, rwkv_a0, rwkv_a_up, rwkv_g_up, rwkv_k_k,
           rwkv_k_a, rwkv_r_k, rwkv_lnx_g, rwkv_lnx_b, fox_f_bias, w_out, norm2_g, moe_w_grp, moe_b_grp,
           moe_w_rt, moe_b_rt, moe_w_gate, moe_w_up, moe_w_down):
    B, S, D = x.shape
    T = B * S
    c = rwkv_w0.shape[0]
    nh_r = c // HEAD_DIM
    n_rwkv = rwkv_mu.shape[0]
    nh_f = fox_f_bias.shape[0]
    c_fox = nh_f * HEAD_DIM
    d_lora = rwkv_w_up.shape[0]
    x2 = x.reshape(T, D)

    w_pad = jnp.pad(w_in, ((0, 0), (0, LANES - nh_f))).astype(BF16)
    p_rwkv, q, k, v, f = _project(x2, mod3, norm1_g.reshape(1, D), w_pad, S, n_rwkv, c_fox, _pick(T, 512))

    wup_pad = jnp.pad(rwkv_w_up, ((0, LANES - d_lora), (0, 0)))
    aup_pad = jnp.pad(rwkv_a_up, ((d_lora, LANES - d_lora - rwkv_a_up.shape[0]), (0, 0)))
    r_t, k_t, v_t, w_t, a_t, g_t = _rwkv_prep(
        p_rwkv, rwkv_mu.reshape(1, n_rwkv), rwkv_w0.reshape(1, c), _split_bf16(wup_pad), rwkv_a0.reshape(1, c),
        _split_bf16(aup_pad), _split_bf16(rwkv_g_up), B, S, c, _pick(S, 256))
    inst = B * nh_r
    to_scan = lambda t: t.reshape(S, inst, HEAD_DIM).transpose(0, 2, 1)
    per_inst = lambda p: jnp.tile(p.reshape(nh_r, HEAD_DIM).T, (1, B))
    scan_in = [to_scan(t) for t in (r_t, k_t, v_t, w_t, a_t)]

    fb_pad = jnp.pad(fox_f_bias, (0, LANES - nh_f)).reshape(1, LANES)
    qa, ka, vt = _fox_prep(q, k, v, f, fb_pad, B, S, nh_f, _pick(S, 512))
    G = nh_f * LANES

    y_scan = _wkv_scan(*scan_in, per_inst(rwkv_k_k), per_inst(rwkv_k_a), per_inst(rwkv_r_k),
                       per_inst(rwkv_lnx_g), per_inst(rwkv_lnx_b), _pick(S, 32))
    y_rwkv = y_scan.transpose(0, 2, 1).reshape(S, B * c)
    y_fox = _fox_attention(qa.reshape(B, S, G), ka.reshape(B, S, G), vt, c_fox, _pick(S, 512)).reshape(T, c_fox)

    w_router = jnp.zeros((D, LANES), F32)
    w_router = w_router.at[:, 0:N_GROUPS].set(moe_w_grp).at[:, E_ROW0:E_ROW0 + N_EXPERTS].set(moe_w_rt)
    b_router = jnp.zeros((1, LANES), F32)
    b_router = b_router.at[0, 0:N_GROUPS].set(moe_b_grp).at[0, E_ROW0:E_ROW0 + N_EXPERTS].set(moe_b_rt)
    x1, h2_words, logits = _mix(y_rwkv, g_t, y_fox, x2, mod3, w_out.astype(BF16), norm2_g.reshape(1, D),
                          _split_bf16(w_router), S, _pick(S, 256))

    ids, wtok, cnt = _route(logits, b_router, _pick(T, 1024))
    counts = cnt[:, 0].astype(I32)
    padded = (counts + EXPERT_ROWS - 1) // EXPERT_ROWS * EXPERT_ROWS
    pends = jnp.cumsum(padded).astype(I32)
    pstarts = pends - padded
    n_rows = (2 * T + N_EXPERTS * (EXPERT_ROWS - 1) + EXPERT_ROWS - 1) // EXPERT_ROWS * EXPERT_ROWS
    n_blocks = n_rows // EXPERT_ROWS
    blk_start = jnp.arange(n_blocks, dtype=I32) * EXPERT_ROWS
    blk_e = jnp.sum((pends[None, :] <= blk_start[:, None]).astype(I32), axis=1)
    blk_e = jnp.minimum(blk_e, N_EXPERTS - 1)
    n_used = pends[-1:] // EXPERT_ROWS

    dest = _dest_rows(ids, pstarts, _pick(T, 1024))
    d0, d1 = dest[0], dest[1]
    xs_words = _dispatch(pstarts, pends, n_used, d0, d1, h2_words, n_rows, _pick(T, 256))
    ys = _experts(blk_e, n_used, xs_words, moe_w_gate, moe_w_up, moe_w_down, EXPERT_ROWS)
    return x1, d0, d1, ys, wtok


def kernel(x, c, w_ada, b_ada, norm1_g, w_in, rwkv_mu, rwkv_w0, rwkv_w_up, rwkv_a0, rwkv_a_up, rwkv_g_up, rwkv_k_k, rwkv_k_a, rwkv_r_k, rwkv_lnx_g, rwkv_lnx_b, fox_f_bias, w_out, norm2_g, moe_w_grp, moe_b_grp, moe_w_rt, moe_b_rt, moe_w_gate, moe_w_up, moe_w_down, norm_f_g):
    B, S, D = x.shape
    assert w_ada.shape[0] == 1, "single-layer model"
    mod3 = _adaln_mod(c, w_ada[0], b_ada[0]).reshape(B, 6, D)
    x1, d0, d1, ys, wtok = _layer(
        x, mod3, norm1_g[0], w_in[0], rwkv_mu[0], rwkv_w0[0], rwkv_w_up[0], rwkv_a0[0], rwkv_a_up[0],
        rwkv_g_up[0], rwkv_k_k[0], rwkv_k_a[0], rwkv_r_k[0], rwkv_lnx_g[0], rwkv_lnx_b[0], fox_f_bias[0],
        w_out[0], norm2_g[0], moe_w_grp[0], moe_b_grp[0], moe_w_rt[0], moe_b_rt[0], moe_w_gate[0],
        moe_w_up[0], moe_w_down[0])
    out = _combine(d0, d1, ys, wtok, x1, mod3, norm_f_g.reshape(1, D), S, _pick(B * S, 256))
    return out.reshape(B, S, D)
```

```python
import functools

import jax
import jax.numpy as jnp
from jax import lax
from jax.experimental import pallas as pl
from jax.experimental.pallas import tpu as pltpu

F32 = jnp.float32
BF16 = jnp.bfloat16
I32 = jnp.int32
HIGHEST = lax.Precision.HIGHEST

HEAD_DIM = 64
N_GROUPS = 4
EXPERTS_PER_GROUP = 8
N_EXPERTS = N_GROUPS * EXPERTS_PER_GROUP
NORM_EPS = 1e-6
GN_EPS = 64e-5
LANES = 128
VMEM_LIMIT = 56 * 1024 * 1024

EXPERT_ROWS = 256
MASK_VALUE = -1e30


def _cparams(sem):
    return pltpu.CompilerParams(dimension_semantics=sem, vmem_limit_bytes=VMEM_LIMIT)


def _sigmoid(x):
    return 1.0 / (1.0 + jnp.exp(-x))


def _softplus(x):
    return jnp.maximum(x, 0.0) + jnp.log(1.0 + jnp.exp(-jnp.abs(x)))


def _split_bf16(w):
    hi = w.astype(BF16)
    return jnp.stack([hi, (w - hi.astype(F32)).astype(BF16)])


def _dot_split(a, w_ref):
    a_hi = a.astype(BF16)
    a_lo = (a - a_hi.astype(F32)).astype(BF16)
    w_hi = w_ref[0]
    return (jnp.dot(a_hi, w_hi, preferred_element_type=F32) + jnp.dot(a_lo, w_hi, preferred_element_type=F32)
            + jnp.dot(a_hi, w_ref[1], preferred_element_type=F32))


def _rms(x):
    return x * lax.rsqrt(jnp.mean(x * x, axis=-1, keepdims=True) + NORM_EPS)


def _mod_kernel(c_ref, w_ref, b_ref, o_ref):
    c = c_ref[...]
    cond = c * _sigmoid(c)
    o_ref[...] = jnp.dot(cond, w_ref[...], precision=HIGHEST, preferred_element_type=F32) + b_ref[...]


def _adaln_mod(c, w_ada, b_ada):
    B, D = c.shape
    N = w_ada.shape[1]
    tn = D
    return pl.pallas_call(
        _mod_kernel,
        grid=(N // tn,),
        in_specs=[pl.BlockSpec((B, D), lambda j: (0, 0)),
                  pl.BlockSpec((D, tn), lambda j: (0, j)),
                  pl.BlockSpec((1, tn), lambda j: (0, j))],
        out_specs=pl.BlockSpec((B, tn), lambda j: (0, j)),
        out_shape=jax.ShapeDtypeStruct((B, N), F32),
        compiler_params=_cparams(("arbitrary",)),
        name="adaln_mod",
    )(c, w_ada, b_ada.reshape(1, N))


def _proj_kernel(x_ref, mod_ref, g_ref, w_ref, pr_ref, q_ref, k_ref, v_ref, f_ref, *, n_rwkv, c_fox):
    m = mod_ref[0]
    h = (_rms(x_ref[...]) * g_ref[...] * (1.0 + m[1:2]) + m[0:1]).astype(BF16)
    o = n_rwkv
    pr_ref[...] = jnp.dot(h, w_ref[:, 0:o], preferred_element_type=F32)
    q = jnp.dot(h, w_ref[:, o:o + c_fox], preferred_element_type=F32)
    q_ref[...] = (q * (LOG2E * HEAD_DIM ** -0.5)).astype(BF16)
    k_ref[...] = jnp.dot(h, w_ref[:, o + c_fox:o + 2 * c_fox], preferred_element_type=F32).astype(BF16)
    v_ref[...] = jnp.dot(h, w_ref[:, o + 2 * c_fox:o + 3 * c_fox], preferred_element_type=F32).astype(BF16)
    f_ref[...] = jnp.dot(h, w_ref[:, o + 3 * c_fox:o + 3 * c_fox + LANES], preferred_element_type=F32)


def _rwkv_prep_kernel(p_ref, mu_ref, w0_ref, wup_ref, a0_ref, aup_ref, gup_ref,
                      r_ref, k_ref, v_ref, w_ref, a_ref, g_ref, carry, *, c):
    p = p_ref[...]
    tt = p.shape[0]
    row = lax.broadcasted_iota(I32, (tt, 1), 0)
    prev = jnp.where(row == 0, carry[...], pltpu.roll(p, 1, 0))
    carry[...] = p[tt - 1:tt, :]
    xs = p + (prev - p) * mu_ref[...]
    r_ref[...] = xs[:, 0:c]
    k_ref[...] = xs[:, c:2 * c]
    v_ref[...] = xs[:, 2 * c:3 * c]
    lo = xs[:, 3 * c:3 * c + LANES]
    wl = _dot_split(jnp.tanh(lo), wup_ref)
    w_log = -_softplus(-(w0_ref[...] + wl)) - 0.5
    w_ref[...] = jnp.exp(-jnp.exp(w_log))
    al = _dot_split(lo, aup_ref)
    a_ref[...] = _sigmoid(a0_ref[...] + al)
    gd = _sigmoid(xs[:, 3 * c + LANES:3 * c + 2 * LANES])
    g_ref[...] = _dot_split(gd, gup_ref)


ROW_UNROLL = 8


def _scan_kernel(r_ref, k_ref, v_ref, w_ref, a_ref, kkc_ref, kac_ref, rkc_ref, lg_ref, lb_ref,
                 y_ref, st, kk_s, b_s, km_s, rr_s, wprev_s, winc_s, yraw):
    @pl.when(pl.program_id(0) == 0)
    def _():
        st[...] = jnp.zeros_like(st)

    tc, n, lanes = r_ref.shape
    k = k_ref[...]
    a = a_ref[...]
    r = r_ref[...]
    kkr = k * kkc_ref[...][None]
    nrm = jnp.sqrt(jnp.sum(kkr * kkr, axis=1, keepdims=True))
    kk = kkr / jnp.maximum(nrm, 1e-12)
    km = k * (1.0 + (a - 1.0) * kac_ref[...][None])
    y_ref[...] = jnp.sum(r * km * rkc_ref[...][None], axis=1, keepdims=True) * v_ref[...]

    def cumulate(t, wc):
        wprev_s[t] = wc
        wc = wc * w_ref[t]
        winc_s[t] = wc
        return wc

    w_chunk = lax.fori_loop(0, tc, cumulate, jnp.ones((n, lanes), F32))
    winc = winc_s[...]
    inv = 1.0 / winc
    kk_s[...] = -(wprev_s[...] * kk)
    b_s[...] = kk * a * inv
    km_s[...] = km * inv
    rr_s[...] = r * winc

    def step(t, carry):
        kk_t = kk_s[t]
        b_t = b_s[t]
        k_t = km_s[t]
        r_t = rr_s[t]

        def rows(j, c2):
            for u in range(ROW_UNROLL):
                vv = j * ROW_UNROLL + u
                s_old = st[vv]
                sa = jnp.sum(s_old * kk_t, axis=0, keepdims=True)
                s_new = s_old + sa * b_t + v_ref[t, pl.ds(vv, 1), :] * k_t
                st[vv] = s_new
                yraw[t, pl.ds(vv, 1), :] = jnp.sum(s_new * r_t, axis=0, keepdims=True)
            return c2

        lax.fori_loop(0, n // ROW_UNROLL, rows, 0, unroll=4)
        return carry

    lax.fori_loop(0, tc, step, 0)
    st[...] = st[...] * w_chunk[None]

    y = yraw[...]
    mean = jnp.mean(y, axis=1, keepdims=True)
    yc = y - mean
    var = jnp.mean(yc * yc, axis=1, keepdims=True)
    yn = yc * lax.rsqrt(var + GN_EPS) * lg_ref[...][None] + lb_ref[...][None]
    y_ref[...] = y_ref[...] + yn


def _wkv_scan(r, k, v, w, a, kkc, kac, rkc, lg, lb, tc):
    S, n, L = r.shape
    blk = pl.BlockSpec((tc, n, L), lambda i: (i, 0, 0))
    cst = pl.BlockSpec((n, L), lambda i: (0, 0))
    return pl.pallas_call(
        _scan_kernel,
        grid=(S // tc,),
        in_specs=[blk] * 5 + [cst] * 5,
        out_specs=blk,
        out_shape=jax.ShapeDtypeStruct((S, n, L), F32),
        scratch_shapes=[pltpu.VMEM((n, n, L), F32)] + [pltpu.VMEM((tc, n, L), F32)] * 7,
        compiler_params=_cparams(("arbitrary",)),
        name="wkv_scan",
    )(r, k, v, w, a, kkc, kac, rkc, lg, lb)


LOG2E = 1.4426950408889634
N_PIECES = 3


def _aux_base(h):
    return (1 - h % 2) * HEAD_DIM


def _fox_prep_kernel(q_ref, k_ref, v_ref, f_ref, fb_ref, tri_ref, sel_ref, mq_ref, cq_ref, ck_ref,
                     qa_ref, ka_ref, vt_ref, carry, *, nh):
    lane1 = lax.broadcasted_iota(I32, (1, LANES), 1)

    def pack_pieces(x):
        pieces, rest = [], x
        for _ in range(N_PIECES):
            piece = rest.astype(BF16).astype(F32)
            pieces.append(piece)
            rest = rest - piece
        packed = jnp.where(lane1 < nh, pieces[0],
                           jnp.where(lane1 < 2 * nh, pltpu.roll(pieces[1], nh, 1),
                                     jnp.where(lane1 < 3 * nh, pltpu.roll(pieces[2], 2 * nh, 1), 0.0)))
        return packed.astype(BF16)

    z = f_ref[...] + fb_ref[...]
    lf = -_softplus(-z)
    part = jnp.dot(tri_ref[...], pack_pieces(lf), preferred_element_type=F32)
    inc = part + pltpu.roll(part, LANES - nh, 1) + pltpu.roll(part, LANES - 2 * nh, 1) + carry[...]
    ts = inc.shape[0]
    carry[...] = inc[ts - 1:ts, :]

    aux = jnp.dot(pack_pieces(inc * LOG2E), sel_ref[...], preferred_element_type=F32)
    aux_q = jnp.where(mq_ref[...] > 0, aux, cq_ref[...])
    aux_k = jnp.where(mq_ref[...] < 0, aux, ck_ref[...])

    lane = lax.broadcasted_iota(I32, (1, nh * LANES), 1)
    own = ((lane // HEAD_DIM) % 2) == ((lane // LANES) % 2)
    q = q_ref[...]
    k = k_ref[...]
    dup = lambda t: jnp.concatenate(
        [t[:, (h // 2) * LANES:(h // 2 + 1) * LANES] for h in range(nh)], axis=1)
    qa_ref[...] = jnp.where(own, dup(q), aux_q.astype(BF16))
    ka_ref[...] = jnp.where(own, dup(k), aux_k.astype(BF16))

    vt = v_ref[...].astype(F32).T
    row = lax.broadcasted_iota(I32, (LANES, 1), 0)
    groups = []
    for h in range(nh):
        pair_rows = vt[(h // 2) * LANES:(h // 2 + 1) * LANES]
        own_rows = (row // HEAD_DIM) == (h % 2)
        ones_row = (row == _aux_base(h)).astype(F32)
        groups.append(jnp.where(own_rows, pair_rows, ones_row))
    vt_ref[0] = jnp.concatenate(groups, axis=0).astype(BF16)


def _fox_prep_constants(nh, ts):
    G = nh * LANES
    tri = (lax.broadcasted_iota(I32, (ts, ts), 0) >= lax.broadcasted_iota(I32, (ts, ts), 1)).astype(BF16)
    sel = jnp.zeros((LANES, G), F32)
    mq = jnp.zeros((1, G), F32)
    cq = jnp.zeros((1, G), F32)
    ck = jnp.zeros((1, G), F32)
    for h in range(nh):
        base = h * LANES + _aux_base(h)
        for pc in range(N_PIECES):
            sel = sel.at[pc * nh + h, base + pc].set(1.0).at[pc * nh + h, base + N_PIECES + pc].set(-1.0)
            mq = mq.at[0, base + pc].set(1.0).at[0, base + N_PIECES + pc].set(-1.0)
            ck = ck.at[0, base + pc].set(1.0)
            cq = cq.at[0, base + N_PIECES + pc].set(1.0)
    return tri, sel.astype(BF16), mq, cq, ck


def _front_kernel(x_ref, mod_ref, ng_ref, w_ref, mu_ref, w0_ref, wup_ref, a0_ref, aup_ref, gup_ref,
                  fb_ref, tri_ref, sel_ref, mq_ref, cq_ref, ck_ref,
                  r_ref, k_ref, v_ref, wd_ref, a_ref, g_ref, qa_ref, ka_ref, vt_ref,
                  pr_s, q_s, kf_s, vf_s, f_s, carry_p, carry_f, *, n_rwkv, c_fox, c, nh, tm):
    @pl.when(pl.program_id(1) == 0)
    def _():
        carry_p[...] = jnp.zeros_like(carry_p)
        carry_f[...] = jnp.zeros_like(carry_f)

    for u in range(x_ref.shape[0] // tm):
        rows = pl.ds(u * tm, tm)
        s = u % 2
        _proj_kernel(x_ref.at[rows], mod_ref, ng_ref, w_ref, pr_s.at[s], q_s.at[s], kf_s.at[s], vf_s.at[s],
                     f_s.at[s], n_rwkv=n_rwkv, c_fox=c_fox)
        _rwkv_prep_kernel(pr_s.at[s], mu_ref, w0_ref, wup_ref, a0_ref, aup_ref, gup_ref,
                          r_ref.at[rows], k_ref.at[rows], v_ref.at[rows], wd_ref.at[rows], a_ref.at[rows],
                          g_ref.at[rows], carry_p, c=c)
        _fox_prep_kernel(q_s.at[s], kf_s.at[s], vf_s.at[s], f_s.at[s], fb_ref, tri_ref, sel_ref, mq_ref,
                         cq_ref, ck_ref, qa_ref.at[rows], ka_ref.at[rows], vt_ref.at[:, :, rows], carry_f, nh=nh)


FRONT_SUBTILES = 2


def _front(x2, mod3, norm_g, w_pad, mu, w0, wup, a0, aup, gup, fb_pad, B, S, n_rwkv, c, c_fox, nh, tm):
    T, D = x2.shape
    NW = w_pad.shape[1]
    G = nh * LANES
    tb = tm * FRONT_SUBTILES if S % (tm * FRONT_SUBTILES) == 0 else tm
    nt = S // tb
    tri, sel, mq, cq, ck = _fox_prep_constants(nh, tm)
    row = lambda b, i: (b * nt + i, 0)
    const = lambda b, i: (0, 0)
    const3 = lambda b, i: (0, 0, 0)
    tmaj = pl.BlockSpec((tb, c), lambda b, i: (i, b))
    return pl.pallas_call(
        functools.partial(_front_kernel, n_rwkv=n_rwkv, c_fox=c_fox, c=c, nh=nh, tm=tm),
        grid=(B, nt),
        in_specs=[pl.BlockSpec((tb, D), row), pl.BlockSpec((1, 6, D), lambda b, i: (b, 0, 0)),
                  pl.BlockSpec((1, D), const), pl.BlockSpec((D, NW), const),
                  pl.BlockSpec((1, n_rwkv), const), pl.BlockSpec((1, c), const),
                  pl.BlockSpec((2, LANES, c), const3), pl.BlockSpec((1, c), const),
                  pl.BlockSpec((2, LANES, c), const3), pl.BlockSpec((2, LANES, c), const3),
                  pl.BlockSpec((1, LANES), const), pl.BlockSpec((tm, tm), const),
                  pl.BlockSpec((LANES, G), const), pl.BlockSpec((1, G), const),
                  pl.BlockSpec((1, G), const), pl.BlockSpec((1, G), const)],
        out_specs=[tmaj] * 5 + [pl.BlockSpec((tb, c), row), pl.BlockSpec((tb, G), row),
                                pl.BlockSpec((tb, G), row), pl.BlockSpec((1, G, tb), lambda b, i: (b, 0, i))],
        out_shape=[jax.ShapeDtypeStruct((S, B * c), F32)] * 5
        + [jax.ShapeDtypeStruct((T, c), F32), jax.ShapeDtypeStruct((T, G), BF16),
           jax.ShapeDtypeStruct((T, G), BF16), jax.ShapeDtypeStruct((B, G, S), BF16)],
        scratch_shapes=[pltpu.VMEM((2, tm, n_rwkv), F32), pltpu.VMEM((2, tm, c_fox), BF16),
                        pltpu.VMEM((2, tm, c_fox), BF16), pltpu.VMEM((2, tm, c_fox), BF16),
                        pltpu.VMEM((2, tm, LANES), F32), pltpu.VMEM((1, n_rwkv), F32), pltpu.VMEM((1, LANES), F32)],
        compiler_params=_cparams(("arbitrary", "arbitrary")),
        name="front_proj_prep",
    )(x2, mod3, norm_g, w_pad, mu, w0, wup, a0, aup, gup, fb_pad, tri, sel, mq, cq, ck)


def _fox_kernel(q_ref, k_ref, vt_ref, o_ref, acc, mrow, s_a, s_b, lim, *, tq):
    qi = pl.program_id(2)
    tk = tq // 2
    n = 2 * (qi + 1)
    acc[...] = jnp.zeros_like(acc)
    mrow[...] = jnp.full_like(mrow, MASK_VALUE)

    @pl.when((pl.program_id(0) == 0) & (pl.program_id(1) == 0) & (qi == 0))
    def _():
        key = lax.broadcasted_iota(I32, (tk, tq), 0)
        query = lax.broadcasted_iota(I32, (tk, tq), 1)
        lim[0] = jnp.where(key <= query, -MASK_VALUE, MASK_VALUE)
        lim[1] = jnp.where(key + tk <= query, -MASK_VALUE, MASK_VALUE)

    def scores(kb, dst):
        off = pl.multiple_of(jnp.minimum(kb, n - 1) * tk, tk)
        for hh in range(2):
            grp = slice(hh * LANES, (hh + 1) * LANES)
            dst[hh] = lax.dot_general(k_ref[0, pl.ds(off, tk), grp], q_ref[0, :, grp],
                                      (((1,), (1,)), ((), ())), preferred_element_type=F32)

    def softmax_pv(kb, src, diag=None):
        off = pl.multiple_of(kb * tk, tk)
        for hh in range(2):
            grp = slice(hh * LANES, (hh + 1) * LANES)
            st = src[hh] if diag is None else jnp.minimum(src[hh], lim[diag])
            m_old = mrow[hh]
            m_new = jnp.maximum(m_old, jnp.max(st, axis=0, keepdims=True))
            mrow[hh] = m_new
            p = jnp.exp2(st - m_new).astype(BF16)
            pv = jnp.dot(vt_ref[0, grp, pl.ds(off, tk)], p, preferred_element_type=F32)
            acc[hh] = jnp.exp2(m_old - m_new) * acc[hh] + pv

    scores(0, s_a)

    def trip(j, c):
        scores(2 * j + 1, s_b)
        softmax_pv(2 * j, s_a)
        scores(2 * j + 2, s_a)
        softmax_pv(2 * j + 1, s_b)
        return c

    lax.fori_loop(0, qi, trip, 0)
    scores(n - 1, s_b)
    softmax_pv(n - 2, s_a, diag=0)
    softmax_pv(n - 1, s_b, diag=1)

    row = lax.broadcasted_iota(I32, (LANES, 1), 0)
    out_t = jnp.zeros((LANES, tq), F32)
    for hh in range(2):
        a = acc[hh]
        base = _aux_base(hh)
        l = a[base:base + 1, :]
        out_t = jnp.where((row // HEAD_DIM) == hh, a / l, out_t)
    o_ref[0] = out_t.T


def _fox_attention(qa, ka, vt, c_fox, tq):
    B, S, G = qa.shape
    nh = G // LANES
    return pl.pallas_call(
        functools.partial(_fox_kernel, tq=tq),
        grid=(B, nh // 2, S // tq),
        in_specs=[pl.BlockSpec((1, tq, 2 * LANES), lambda b, p, i: (b, i, p)),
                  pl.BlockSpec((1, S, 2 * LANES), lambda b, p, i: (b, 0, p)),
                  pl.BlockSpec((1, 2 * LANES, S), lambda b, p, i: (b, p, 0))],
        out_specs=pl.BlockSpec((1, tq, LANES), lambda b, p, i: (b, i, p)),
        out_shape=jax.ShapeDtypeStruct((B, S, c_fox), F32),
        scratch_shapes=[pltpu.VMEM((2, LANES, tq), F32), pltpu.VMEM((2, 1, tq), F32),
                        pltpu.VMEM((2, tq // 2, tq), F32), pltpu.VMEM((2, tq // 2, tq), F32),
                        pltpu.VMEM((2, tq // 2, tq), F32)],
        compiler_params=_cparams(("arbitrary", "arbitrary", "arbitrary")),
        name="fox_attention",
    )(qa, ka, vt)


def _mix_kernel(yr_ref, g_ref, yf_ref, x_ref, mod_ref, wo_ref, n2_ref, wr_ref,
                x1_ref, h2_ref, lg_ref, *, c):
    m = mod_ref[0]
    a = (yr_ref[...] * g_ref[...]).astype(BF16)
    b = yf_ref[...].astype(BF16)
    mix = (jnp.dot(a, wo_ref[0:c, :], preferred_element_type=F32)
           + jnp.dot(b, wo_ref[c:, :], preferred_element_type=F32))
    x1 = x_ref[...] + m[2:3] * mix
    x1_ref[...] = x1
    h2 = _rms(x1) * n2_ref[...] * (1.0 + m[4:5]) + m[3:4]
    h2_ref[...] = _pack_bf16_halves(h2)
    lg_ref[...] = _dot_split(h2, wr_ref)


def _pack_bf16_halves(x):
    w = x.shape[1] // 2
    bits = lax.bitcast_convert_type(x.astype(BF16).astype(F32), jnp.uint32)
    return (bits[:, :w] >> 16) | (bits[:, w:] & jnp.uint32(0xFFFF0000))


def _unpack_bf16_halves(words):
    lo = lax.bitcast_convert_type(words << 16, F32)
    hi = lax.bitcast_convert_type(words & jnp.uint32(0xFFFF0000), F32)
    return jnp.concatenate([lo, hi], axis=1).astype(BF16)


def _mix(y_rwkv, g, y_fox, x2, mod3, w_out, norm2_g, w_router, S, tm):
    T, D = x2.shape
    c = g.shape[1]
    nt = S // tm
    row = lambda i: (i, 0)
    const = lambda i: (0, 0)
    return pl.pallas_call(
        functools.partial(_mix_kernel, c=c),
        grid=(T // tm,),
        in_specs=[pl.BlockSpec((tm, c), lambda i: (i % nt, i // nt)),
                  pl.BlockSpec((tm, c), row), pl.BlockSpec((tm, c), row),
                  pl.BlockSpec((tm, D), row),
                  pl.BlockSpec((1, 6, D), lambda i: ((i * tm) // S, 0, 0)),
                  pl.BlockSpec((D, D), const), pl.BlockSpec((1, D), const),
                  pl.BlockSpec((2, D, LANES), lambda i: (0, 0, 0))],
        out_specs=[pl.BlockSpec((tm, D), row), pl.BlockSpec((tm, D // 2), row), pl.BlockSpec((tm, LANES), row)],
        out_shape=[jax.ShapeDtypeStruct((T, D), F32), jax.ShapeDtypeStruct((T, D // 2), jnp.uint32),
                   jax.ShapeDtypeStruct((T, LANES), F32)],
        compiler_params=_cparams(("arbitrary",)),
        name="out_proj_norm2_router",
    )(y_rwkv, g, y_fox, x2, mod3, w_out, norm2_g, w_router)


E_ROW0 = 8


def _first_argmax(vals, n):
    mx = jnp.max(vals, axis=0, keepdims=True)
    idx = lax.broadcasted_iota(I32, vals.shape, 0).astype(F32)
    first = jnp.min(jnp.where(vals == mx, idx, float(n)), axis=0, keepdims=True)
    return first.astype(I32), mx


def _route_kernel(lg_ref, bias_ref, tri_ref, ids_ref, wtok_ref, cnt_ref, carry):
    @pl.when(pl.program_id(0) == 0)
    def _():
        carry[...] = jnp.zeros_like(carry)

    lt = (lg_ref[...] + bias_ref[...]).T
    tm = lt.shape[1]
    grp = lt[0:N_GROUPS]
    ge = jnp.exp(grp - jnp.max(grp, axis=0, keepdims=True))
    gp = ge / jnp.sum(ge, axis=0, keepdims=True)
    g_sel, p_g = _first_argmax(gp, N_GROUPS)
    sel = jnp.zeros((EXPERTS_PER_GROUP, tm), F32)
    for g in range(N_GROUPS):
        lo = E_ROW0 + g * EXPERTS_PER_GROUP
        sel = jnp.where(g_sel == g, lt[lo:lo + EXPERTS_PER_GROUP], sel)
    ee = jnp.exp(sel - jnp.max(sel, axis=0, keepdims=True))
    ep = ee / jnp.sum(ee, axis=0, keepdims=True)
    i0, p0 = _first_argmax(ep, EXPERTS_PER_GROUP)
    idx8 = lax.broadcasted_iota(I32, ep.shape, 0)
    i1, p1 = _first_argmax(jnp.where(idx8 == i0, -1.0, ep), EXPERTS_PER_GROUP)
    den = p0 + p1
    w0 = p_g * p0 / den
    w1 = p_g * p1 / den
    e0 = g_sel * EXPERTS_PER_GROUP + i0
    e1 = g_sel * EXPERTS_PER_GROUP + i1

    ide = lax.broadcasted_iota(I32, (N_EXPERTS, tm), 0)
    oh0 = ide == e0
    oh1 = ide == e1
    oh = oh0.astype(F32) + oh1.astype(F32)
    incl = jnp.dot(oh.astype(BF16), tri_ref[...], preferred_element_type=F32)
    base = carry[...] + (incl - oh)
    r0 = jnp.sum(jnp.where(oh0, base, 0.0), axis=0, keepdims=True)
    r1 = jnp.sum(jnp.where(oh1, base, 0.0), axis=0, keepdims=True)
    carry[...] = carry[...] + incl[:, tm - 1:tm]
    cnt_ref[...] = jnp.broadcast_to(carry[...], cnt_ref.shape)
    ids_ref[...] = jnp.concatenate(
        [e0, e1, r0.astype(I32), r1.astype(I32), jnp.zeros((4, tm), I32)], axis=0)
    wtok_ref[...] = jnp.concatenate([w0, w1, jnp.zeros((LANES - 2, tm), F32)], axis=0).T


def _route(logits, bias_row, tm):
    T = logits.shape[0]
    tri = (lax.broadcasted_iota(I32, (tm, tm), 0) <= lax.broadcasted_iota(I32, (tm, tm), 1)).astype(BF16)
    return pl.pallas_call(
        _route_kernel,
        grid=(T // tm,),
        in_specs=[pl.BlockSpec((tm, LANES), lambda i: (i, 0)),
                  pl.BlockSpec((1, LANES), lambda i: (0, 0)),
                  pl.BlockSpec((tm, tm), lambda i: (0, 0))],
        out_specs=[pl.BlockSpec((8, tm), lambda i: (0, i)),
                   pl.BlockSpec((tm, LANES), lambda i: (i, 0)),
                   pl.BlockSpec((N_EXPERTS, LANES), lambda i: (0, 0))],
        out_shape=[jax.ShapeDtypeStruct((8, T), I32), jax.ShapeDtypeStruct((T, LANES), F32),
                   jax.ShapeDtypeStruct((N_EXPERTS, LANES), F32)],
        scratch_shapes=[pltpu.VMEM((N_EXPERTS, 1), F32)],
        compiler_params=_cparams(("arbitrary",)),
        name="route_rank",
    )(logits, bias_row, tri)


def _dest_kernel(ids_ref, ps_ref, d_ref):
    ids = ids_ref[...]
    tm = ids.shape[1]
    ide = lax.broadcasted_iota(I32, (N_EXPERTS, tm), 0)
    ps = ps_ref[...]
    rows = [jnp.sum(jnp.where(ide == ids[k:k + 1], ps, 0), axis=0, keepdims=True) + ids[2 + k:3 + k]
            for k in range(2)]
    d_ref[...] = jnp.concatenate(rows + [jnp.zeros((6, tm), I32)], axis=0)


def _dest_rows(ids, pstarts, tm):
    T = ids.shape[1]
    return pl.pallas_call(
        _dest_kernel,
        grid=(T // tm,),
        in_specs=[pl.BlockSpec((8, tm), lambda i: (0, i)), pl.BlockSpec((N_EXPERTS, 1), lambda i: (0, 0))],
        out_specs=pl.BlockSpec((8, tm), lambda i: (0, i)),
        out_shape=jax.ShapeDtypeStruct((8, T), I32),
        compiler_params=_cparams(("arbitrary",)),
        name="dest_rows",
    )(ids, pstarts.reshape(N_EXPERTS, 1))


def _row_copy(src, dst, sem):
    return pltpu.make_async_copy(src, dst, sem)


def _dispatch_kernel(ps_ref, pe_ref, nu_ref, d0_ref, d1_ref, h_ref, xs_out, zbuf, sem, zsem, *, n_blocks, n_tail):
    tmd = h_ref.shape[0]
    er = zbuf.shape[0]

    @pl.when(pl.program_id(0) == 0)
    def _():
        zbuf[...] = jnp.zeros_like(zbuf)

        def pad_copy(e):
            return _row_copy(zbuf, xs_out.at[pl.ds(pl.multiple_of(pe_ref[e] - er, er), er)], zsem)

        def tail_copy(j):
            return _row_copy(zbuf, xs_out.at[pl.ds(pl.multiple_of((nu_ref[0] + j) * er, er), er)], zsem)

        def each(fn):
            def pads(e, c):
                @pl.when(pe_ref[e] > ps_ref[e])
                def _():
                    fn(pad_copy(e))
                return c

            def tails(j, c):
                @pl.when(nu_ref[0] + j < n_blocks)
                def _():
                    fn(tail_copy(j))
                return c

            lax.fori_loop(0, N_EXPERTS, pads, 0)
            lax.fori_loop(0, n_tail, tails, 0)

        each(lambda cp: cp.start())
        each(lambda cp: cp.wait())

    def issue(j, c):
        _row_copy(h_ref.at[pl.ds(j, 1)], xs_out.at[pl.ds(d0_ref[j], 1)], sem).start()
        _row_copy(h_ref.at[pl.ds(j, 1)], xs_out.at[pl.ds(d1_ref[j], 1)], sem).start()
        return c

    lax.fori_loop(0, tmd, issue, 0, unroll=ISSUE_UNROLL)

    def drain(j, c):
        _row_copy(h_ref.at[pl.ds(0, 1)], xs_out.at[pl.ds(0, 1)], sem).wait()
        _row_copy(h_ref.at[pl.ds(0, 1)], xs_out.at[pl.ds(0, 1)], sem).wait()
        return c

    lax.fori_loop(0, tmd, drain, 0)


ISSUE_UNROLL = 4


def _dispatch(pstarts, pends, n_used, d0, d1, h2_words, n_rows, tmd):
    T, W = h2_words.shape
    n_blocks = n_rows // EXPERT_ROWS
    n_tail = n_blocks - (2 * T) // EXPERT_ROWS
    grid_spec = pltpu.PrefetchScalarGridSpec(
        num_scalar_prefetch=3,
        grid=(T // tmd,),
        in_specs=[pl.BlockSpec((tmd,), lambda i, *_: (i,), memory_space=pltpu.SMEM),
                  pl.BlockSpec((tmd,), lambda i, *_: (i,), memory_space=pltpu.SMEM),
                  pl.BlockSpec((tmd, W), lambda i, *_: (i, 0))],
        out_specs=pl.BlockSpec(memory_space=pl.ANY),
        scratch_shapes=[pltpu.VMEM((EXPERT_ROWS, W), h2_words.dtype),
                        pltpu.SemaphoreType.DMA(()), pltpu.SemaphoreType.DMA(())],
    )
    return pl.pallas_call(
        functools.partial(_dispatch_kernel, n_blocks=n_blocks, n_tail=n_tail),
        grid_spec=grid_spec,
        out_shape=jax.ShapeDtypeStruct((n_rows, W), h2_words.dtype),
        compiler_params=_cparams(("arbitrary",)),
        name="moe_dispatch",
    )(pstarts, pends, n_used, d0, d1, h2_words)


def _expert_kernel(be_ref, nu_ref, x_ref, wg_ref, wu_ref, wd_ref, y_ref, wg_s, wu_s, wd_s):
    i = pl.program_id(0)
    used = i < nu_ref[0]

    @pl.when(used & ((i == 0) | (be_ref[i] != be_ref[jnp.maximum(i - 1, 0)])))
    def _():
        wg_s[...] = wg_ref[0].astype(BF16)
        wu_s[...] = wu_ref[0].astype(BF16)
        wd_s[...] = wd_ref[0].astype(BF16)

    @pl.when(used)
    def _():
        x = _unpack_bf16_halves(x_ref[...])
        g = jnp.dot(x, wg_s[...], preferred_element_type=F32)
        u = jnp.dot(x, wu_s[...], preferred_element_type=F32)
        hid = (g * _sigmoid(g) * u).astype(BF16)
        y_ref[...] = jnp.dot(hid, wd_s[...], preferred_element_type=F32)

    @pl.when(jnp.logical_not(used))
    def _():
        y_ref[...] = jnp.zeros_like(y_ref)


def _experts(blk_e, n_used, xs_words, wg, wu, wd, tme):
    P, W = xs_words.shape
    _, D, F = wg.shape
    grid_spec = pltpu.PrefetchScalarGridSpec(
        num_scalar_prefetch=2,
        grid=(P // tme,),
        in_specs=[pl.BlockSpec((tme, W), lambda i, be, nu: (i, 0)),
                  pl.BlockSpec((1, D, F), lambda i, be, nu: (be[i], 0, 0)),
                  pl.BlockSpec((1, D, F), lambda i, be, nu: (be[i], 0, 0)),
                  pl.BlockSpec((1, F, D), lambda i, be, nu: (be[i], 0, 0))],
        out_specs=pl.BlockSpec((tme, D), lambda i, be, nu: (i, 0)),
        scratch_shapes=[pltpu.VMEM((D, F), BF16), pltpu.VMEM((D, F), BF16), pltpu.VMEM((F, D), BF16)],
    )
    return pl.pallas_call(
        _expert_kernel,
        grid_spec=grid_spec,
        out_shape=jax.ShapeDtypeStruct((P, D), F32),
        compiler_params=_cparams(("arbitrary",)),
        name="moe_experts",
    )(blk_e, n_used, xs_words, wg, wu, wd)


def _combine_kernel(d0c_ref, d1c_ref, d0n_ref, d1n_ref, ys_ref, wtok_ref, x1_ref, mod_ref, gf_ref, o_ref,
                    ybuf, sems):
    tmc = x1_ref.shape[0]
    i = pl.program_id(0)
    slot = i % 2

    def gather(d0_ref, d1_ref, s):
        def issue(j, c):
            _row_copy(ys_ref.at[pl.ds(d0_ref[j], 1)], ybuf.at[s, 0, pl.ds(j, 1)], sems.at[s]).start()
            _row_copy(ys_ref.at[pl.ds(d1_ref[j], 1)], ybuf.at[s, 1, pl.ds(j, 1)], sems.at[s]).start()
            return c

        lax.fori_loop(0, tmc, issue, 0, unroll=ISSUE_UNROLL)

    @pl.when(i == 0)
    def _():
        gather(d0c_ref, d1c_ref, 0)

    @pl.when(i + 1 < pl.num_programs(0))
    def _():
        gather(d0n_ref, d1n_ref, 1 - slot)

    def drain(j, c):
        _row_copy(ys_ref.at[pl.ds(0, 1)], ybuf.at[slot, 0, pl.ds(0, 1)], sems.at[slot]).wait()
        _row_copy(ys_ref.at[pl.ds(0, 1)], ybuf.at[slot, 1, pl.ds(0, 1)], sems.at[slot]).wait()
        return c

    lax.fori_loop(0, tmc, drain, 0)

    m = mod_ref[0]
    w = wtok_ref[...]
    ff = w[:, 0:1] * ybuf[slot, 0] + w[:, 1:2] * ybuf[slot, 1]
    x2 = x1_ref[...] + m[5:6] * ff
    o_ref[...] = _rms(x2) * gf_ref[...]


def _combine(d0, d1, ys, wtok, x1, mod3, norm_f_g, S, tmc):
    T, D = x1.shape
    n = T // tmc
    cur = lambda i: (i,)
    nxt = lambda i: (jnp.minimum(i + 1, n - 1),)
    smem = lambda f: pl.BlockSpec((tmc,), f, memory_space=pltpu.SMEM)
    return pl.pallas_call(
        _combine_kernel,
        grid=(n,),
        in_specs=[smem(cur), smem(cur), smem(nxt), smem(nxt),
                  pl.BlockSpec(memory_space=pl.ANY),
                  pl.BlockSpec((tmc, LANES), lambda i: (i, 0)),
                  pl.BlockSpec((tmc, D), lambda i: (i, 0)),
                  pl.BlockSpec((1, 6, D), lambda i: ((i * tmc) // S, 0, 0)),
                  pl.BlockSpec((1, D), lambda i: (0, 0))],
        out_specs=pl.BlockSpec((tmc, D), lambda i: (i, 0)),
        out_shape=jax.ShapeDtypeStruct((T, D), F32),
        scratch_shapes=[pltpu.VMEM((2, 2, tmc, D), F32), pltpu.SemaphoreType.DMA((2,))],
        compiler_params=_cparams(("arbitrary",)),
        name="moe_combine_final_norm",
    )(d0, d1, d0, d1, ys, wtok, x1, mod3, norm_f_g)


def _pick(n, pref):
    t = min(pref, n)
    while n % t:
        t //= 2
    return t


def _layer(x, mod3, norm1_g, w_in, rwkv_mu, rwkv_w0, rwkv_w_up, rwkv_a0, rwkv_a_up, rwkv_g_up, rwkv_k_k,
           rwkv_k_a, rwkv_r_k, rwkv_lnx_g, rwkv_lnx_b, fox_f_bias, w_out, norm2_g, moe_w_grp, moe_b_grp,
           moe_w_rt, moe_b_rt, moe_w_gate, moe_w_up, moe_w_down):
    B, S, D = x.shape
    T = B * S
    c = rwkv_w0.shape[0]
    nh_r = c // HEAD_DIM
    n_rwkv = rwkv_mu.shape[0]
    nh_f = fox_f_bias.shape[0]
    c_fox = nh_f * HEAD_DIM
    d_lora = rwkv_w_up.shape[0]
    x2 = x.reshape(T, D)

    w_pad = jnp.pad(w_in, ((0, 0), (0, LANES - nh_f))).astype(BF16)
    wup_pad = jnp.pad(rwkv_w_up, ((0, LANES - d_lora), (0, 0)))
    aup_pad = jnp.pad(rwkv_a_up, ((d_lora, LANES - d_lora - rwkv_a_up.shape[0]), (0, 0)))
    fb_pad = jnp.pad(fox_f_bias, (0, LANES - nh_f)).reshape(1, LANES)
    r_t, k_t, v_t, w_t, a_t, g_t, qa, ka, vt = _front(
        x2, mod3, norm1_g.reshape(1, D), w_pad, rwkv_mu.reshape(1, n_rwkv), rwkv_w0.reshape(1, c),
        _split_bf16(wup_pad), rwkv_a0.reshape(1, c), _split_bf16(aup_pad), _split_bf16(rwkv_g_up), fb_pad,
        B, S, n_rwkv, c, c_fox, nh_f, _pick(S, 256))
    G = nh_f * LANES

    inst = B * nh_r
    to_scan = lambda t: t.reshape(S, inst, HEAD_DIM).transpose(0, 2, 1)
    per_inst = lambda p: jnp.tile(p.reshape(nh_r, HEAD_DIM).T, (1, B))
    scan_in = [to_scan(t) for t in (r_t, k_t, v_t, w_t, a_t)]
    y_scan = _wkv_scan(*scan_in, per_inst(rwkv_k_k), per_inst(rwkv_k_a), per_inst(rwkv_r_k),
                       per_inst(rwkv_lnx_g), per_inst(rwkv_lnx_b), _pick(S, 32))
    y_rwkv = y_scan.transpose(0, 2, 1).reshape(S, B * c)
    y_fox = _fox_attention(qa.reshape(B, S, G), ka.reshape(B, S, G), vt, c_fox, _pick(S, 512)).reshape(T, c_fox)

    w_router = jnp.zeros((D, LANES), F32)
    w_router = w_router.at[:, 0:N_GROUPS].set(moe_w_grp).at[:, E_ROW0:E_ROW0 + N_EXPERTS].set(moe_w_rt)
    b_router = jnp.zeros((1, LANES), F32)
    b_router = b_router.at[0, 0:N_GROUPS].set(moe_b_grp).at[0, E_ROW0:E_ROW0 + N_EXPERTS].set(moe_b_rt)
    x1, h2_words, logits = _mix(y_rwkv, g_t, y_fox, x2, mod3, w_out.astype(BF16), norm2_g.reshape(1, D),
                          _split_bf16(w_router), S, _pick(S, 256))

    ids, wtok, cnt = _route(logits, b_router, _pick(T, 1024))
    counts = cnt[:, 0].astype(I32)
    padded = (counts + EXPERT_ROWS - 1) // EXPERT_ROWS * EXPERT_ROWS
    pends = jnp.cumsum(padded).astype(I32)
    pstarts = pends - padded
    n_rows = (2 * T + N_EXPERTS * (EXPERT_ROWS - 1) + EXPERT_ROWS - 1) // EXPERT_ROWS * EXPERT_ROWS
    n_blocks = n_rows // EXPERT_ROWS
    blk_start = jnp.arange(n_blocks, dtype=I32) * EXPERT_ROWS
    blk_e = jnp.sum((pends[None, :] <= blk_start[:, None]).astype(I32), axis=1)
    blk_e = jnp.minimum(blk_e, N_EXPERTS - 1)
    n_used = pends[-1:] // EXPERT_ROWS

    dest = _dest_rows(ids, pstarts, _pick(T, 1024))
    d0, d1 = dest[0], dest[1]
    xs_words = _dispatch(pstarts, pends, n_used, d0, d1, h2_words, n_rows, _pick(T, 256))
    ys = _experts(blk_e, n_used, xs_words, moe_w_gate, moe_w_up, moe_w_down, EXPERT_ROWS)
    return x1, d0, d1, ys, wtok


def kernel(x, c, w_ada, b_ada, norm1_g, w_in, rwkv_mu, rwkv_w0, rwkv_w_up, rwkv_a0, rwkv_a_up, rwkv_g_up, rwkv_k_k, rwkv_k_a, rwkv_r_k, rwkv_lnx_g, rwkv_lnx_b, fox_f_bias, w_out, norm2_g, moe_w_grp, moe_b_grp, moe_w_rt, moe_b_rt, moe_w_gate, moe_w_up, moe_w_down, norm_f_g):
    B, S, D = x.shape
    assert w_ada.shape[0] == 1, "single-layer model"
    mod3 = _adaln_mod(c, w_ada[0], b_ada[0]).reshape(B, 6, D)
    x1, d0, d1, ys, wtok = _layer(
        x, mod3, norm1_g[0], w_in[0], rwkv_mu[0], rwkv_w0[0], rwkv_w_up[0], rwkv_a0[0], rwkv_a_up[0],
        rwkv_g_up[0], rwkv_k_k[0], rwkv_k_a[0], rwkv_r_k[0], rwkv_lnx_g[0], rwkv_lnx_b[0], fox_f_bias[0],
        w_out[0], norm2_g[0], moe_w_grp[0], moe_b_grp[0], moe_w_rt[0], moe_b_rt[0], moe_w_gate[0],
        moe_w_up[0], moe_w_down[0])
    out = _combine(d0, d1, ys, wtok, x1, mod3, norm_f_g.reshape(1, D), S, _pick(B * S, 256))
    return out.reshape(B, S, D)
```

```python
import functools

import jax
import jax.numpy as jnp
from jax import lax
from jax.experimental import pallas as pl
from jax.experimental.pallas import tpu as pltpu
from jax.experimental.pallas import tpu_sc as plsc

F32 = jnp.float32
BF16 = jnp.bfloat16
I32 = jnp.int32
HIGHEST = lax.Precision.HIGHEST

HEAD_DIM = 64
N_GROUPS = 4
EXPERTS_PER_GROUP = 8
N_EXPERTS = N_GROUPS * EXPERTS_PER_GROUP
NORM_EPS = 1e-6
GN_EPS = 64e-5
LANES = 128
VMEM_LIMIT = 56 * 1024 * 1024

EXPERT_ROWS = 256
MASK_VALUE = -1e30


def _cparams(sem):
    return pltpu.CompilerParams(dimension_semantics=sem, vmem_limit_bytes=VMEM_LIMIT)


def _sigmoid(x):
    return 1.0 / (1.0 + jnp.exp(-x))


def _softplus(x):
    return jnp.maximum(x, 0.0) + jnp.log(1.0 + jnp.exp(-jnp.abs(x)))


def _split_bf16(w):
    hi = w.astype(BF16)
    return jnp.stack([hi, (w - hi.astype(F32)).astype(BF16)])


def _dot_split(a, w_ref):
    a_hi = a.astype(BF16)
    a_lo = (a - a_hi.astype(F32)).astype(BF16)
    w_hi = w_ref[0]
    return (jnp.dot(a_hi, w_hi, preferred_element_type=F32) + jnp.dot(a_lo, w_hi, preferred_element_type=F32)
            + jnp.dot(a_hi, w_ref[1], preferred_element_type=F32))


def _rms(x):
    return x * lax.rsqrt(jnp.mean(x * x, axis=-1, keepdims=True) + NORM_EPS)


def _mod_kernel(c_ref, w_ref, b_ref, o_ref):
    c = c_ref[...]
    cond = c * _sigmoid(c)
    o_ref[...] = jnp.dot(cond, w_ref[...], precision=HIGHEST, preferred_element_type=F32) + b_ref[...]


def _adaln_mod(c, w_ada, b_ada):
    B, D = c.shape
    N = w_ada.shape[1]
    tn = D
    return pl.pallas_call(
        _mod_kernel,
        grid=(N // tn,),
        in_specs=[pl.BlockSpec((B, D), lambda j: (0, 0)),
                  pl.BlockSpec((D, tn), lambda j: (0, j)),
                  pl.BlockSpec((1, tn), lambda j: (0, j))],
        out_specs=pl.BlockSpec((B, tn), lambda j: (0, j)),
        out_shape=jax.ShapeDtypeStruct((B, N), F32),
        compiler_params=_cparams(("arbitrary",)),
        name="adaln_mod",
    )(c, w_ada, b_ada.reshape(1, N))


def _proj_kernel(x_ref, mod_ref, g_ref, w_ref, pr_ref, q_ref, k_ref, v_ref, f_ref, *, n_rwkv, c_fox):
    m = mod_ref[0]
    h = (_rms(x_ref[...]) * g_ref[...] * (1.0 + m[1:2]) + m[0:1]).astype(BF16)
    o = n_rwkv
    pr_ref[...] = jnp.dot(h, w_ref[:, 0:o], preferred_element_type=F32)
    q = jnp.dot(h, w_ref[:, o:o + c_fox], preferred_element_type=F32)
    q_ref[...] = (q * (LOG2E * HEAD_DIM ** -0.5)).astype(BF16)
    k_ref[...] = jnp.dot(h, w_ref[:, o + c_fox:o + 2 * c_fox], preferred_element_type=F32).astype(BF16)
    v_ref[...] = jnp.dot(h, w_ref[:, o + 2 * c_fox:o + 3 * c_fox], preferred_element_type=F32).astype(BF16)
    f_ref[...] = jnp.dot(h, w_ref[:, o + 3 * c_fox:o + 3 * c_fox + LANES], preferred_element_type=F32)


def _rwkv_prep_kernel(p_ref, mu_ref, w0_ref, wup_ref, a0_ref, aup_ref, gup_ref,
                      r_ref, k_ref, v_ref, w_ref, a_ref, g_ref, carry, *, c):
    p = p_ref[...]
    tt = p.shape[0]
    row = lax.broadcasted_iota(I32, (tt, 1), 0)
    prev = jnp.where(row == 0, carry[...], pltpu.roll(p, 1, 0))
    carry[...] = p[tt - 1:tt, :]
    xs = p + (prev - p) * mu_ref[...]
    r_ref[...] = xs[:, 0:c]
    k_ref[...] = xs[:, c:2 * c]
    v_ref[...] = xs[:, 2 * c:3 * c]
    lo = xs[:, 3 * c:3 * c + LANES]
    wl = _dot_split(jnp.tanh(lo), wup_ref)
    w_log = -_softplus(-(w0_ref[...] + wl)) - 0.5
    w_ref[...] = jnp.exp(-jnp.exp(w_log))
    al = _dot_split(lo, aup_ref)
    a_ref[...] = _sigmoid(a0_ref[...] + al)
    gd = _sigmoid(xs[:, 3 * c + LANES:3 * c + 2 * LANES])
    g_ref[...] = _dot_split(gd, gup_ref)


ROW_UNROLL = 8


def _scan_kernel(r_ref, k_ref, v_ref, w_ref, a_ref, kkc_ref, kac_ref, rkc_ref, lg_ref, lb_ref,
                 y_ref, st, kk_s, b_s, km_s, rr_s, wprev_s, winc_s, yraw):
    @pl.when(pl.program_id(0) == 0)
    def _():
        st[...] = jnp.zeros_like(st)

    tc, n, lanes = r_ref.shape
    k = k_ref[...]
    a = a_ref[...]
    r = r_ref[...]
    kkr = k * kkc_ref[...][None]
    nrm = jnp.sqrt(jnp.sum(kkr * kkr, axis=1, keepdims=True))
    kk = kkr / jnp.maximum(nrm, 1e-12)
    km = k * (1.0 + (a - 1.0) * kac_ref[...][None])
    y_ref[...] = jnp.sum(r * km * rkc_ref[...][None], axis=1, keepdims=True) * v_ref[...]

    def cumulate(t, wc):
        wprev_s[t] = wc
        wc = wc * w_ref[t]
        winc_s[t] = wc
        return wc

    w_chunk = lax.fori_loop(0, tc, cumulate, jnp.ones((n, lanes), F32))
    winc = winc_s[...]
    inv = 1.0 / winc
    kk_s[...] = -(wprev_s[...] * kk)
    b_s[...] = kk * a * inv
    km_s[...] = km * inv
    rr_s[...] = r * winc

    def step(t, carry):
        kk_t = kk_s[t]
        b_t = b_s[t]
        k_t = km_s[t]
        r_t = rr_s[t]

        def rows(j, c2):
            for u in range(ROW_UNROLL):
                vv = j * ROW_UNROLL + u
                s_old = st[vv]
                sa = jnp.sum(s_old * kk_t, axis=0, keepdims=True)
                s_new = s_old + sa * b_t + v_ref[t, pl.ds(vv, 1), :] * k_t
                st[vv] = s_new
                yraw[t, pl.ds(vv, 1), :] = jnp.sum(s_new * r_t, axis=0, keepdims=True)
            return c2

        lax.fori_loop(0, n // ROW_UNROLL, rows, 0, unroll=4)
        return carry

    lax.fori_loop(0, tc, step, 0)
    st[...] = st[...] * w_chunk[None]

    y = yraw[...]
    mean = jnp.mean(y, axis=1, keepdims=True)
    yc = y - mean
    var = jnp.mean(yc * yc, axis=1, keepdims=True)
    yn = yc * lax.rsqrt(var + GN_EPS) * lg_ref[...][None] + lb_ref[...][None]
    y_ref[...] = y_ref[...] + yn


def _wkv_scan(r, k, v, w, a, kkc, kac, rkc, lg, lb, tc):
    S, n, L = r.shape
    blk = pl.BlockSpec((tc, n, L), lambda i: (i, 0, 0))
    cst = pl.BlockSpec((n, L), lambda i: (0, 0))
    return pl.pallas_call(
        _scan_kernel,
        grid=(S // tc,),
        in_specs=[blk] * 5 + [cst] * 5,
        out_specs=blk,
        out_shape=jax.ShapeDtypeStruct((S, n, L), F32),
        scratch_shapes=[pltpu.VMEM((n, n, L), F32)] + [pltpu.VMEM((tc, n, L), F32)] * 7,
        compiler_params=_cparams(("arbitrary",)),
        name="wkv_scan",
    )(r, k, v, w, a, kkc, kac, rkc, lg, lb)


LOG2E = 1.4426950408889634
N_PIECES = 3


def _aux_base(h):
    return (1 - h % 2) * HEAD_DIM


def _fox_prep_kernel(q_ref, k_ref, v_ref, f_ref, fb_ref, tri_ref, sel_ref, mq_ref, cq_ref, ck_ref,
                     qa_ref, ka_ref, vt_ref, carry, *, nh):
    lane1 = lax.broadcasted_iota(I32, (1, LANES), 1)

    def pack_pieces(x):
        pieces, rest = [], x
        for _ in range(N_PIECES):
            piece = rest.astype(BF16).astype(F32)
            pieces.append(piece)
            rest = rest - piece
        packed = jnp.where(lane1 < nh, pieces[0],
                           jnp.where(lane1 < 2 * nh, pltpu.roll(pieces[1], nh, 1),
                                     jnp.where(lane1 < 3 * nh, pltpu.roll(pieces[2], 2 * nh, 1), 0.0)))
        return packed.astype(BF16)

    z = f_ref[...] + fb_ref[...]
    lf = -_softplus(-z)
    part = jnp.dot(tri_ref[...], pack_pieces(lf), preferred_element_type=F32)
    inc = part + pltpu.roll(part, LANES - nh, 1) + pltpu.roll(part, LANES - 2 * nh, 1) + carry[...]
    ts = inc.shape[0]
    carry[...] = inc[ts - 1:ts, :]

    aux = jnp.dot(pack_pieces(inc * LOG2E), sel_ref[...], preferred_element_type=F32)
    aux_q = jnp.where(mq_ref[...] > 0, aux, cq_ref[...])
    aux_k = jnp.where(mq_ref[...] < 0, aux, ck_ref[...])

    lane = lax.broadcasted_iota(I32, (1, nh * LANES), 1)
    own = ((lane // HEAD_DIM) % 2) == ((lane // LANES) % 2)
    q = q_ref[...]
    k = k_ref[...]
    dup = lambda t: jnp.concatenate(
        [t[:, (h // 2) * LANES:(h // 2 + 1) * LANES] for h in range(nh)], axis=1)
    qa_ref[...] = jnp.where(own, dup(q), aux_q.astype(BF16))
    ka_ref[...] = jnp.where(own, dup(k), aux_k.astype(BF16))

    vt = v_ref[...].astype(F32).T
    row = lax.broadcasted_iota(I32, (LANES, 1), 0)
    groups = []
    for h in range(nh):
        pair_rows = vt[(h // 2) * LANES:(h // 2 + 1) * LANES]
        own_rows = (row // HEAD_DIM) == (h % 2)
        ones_row = (row == _aux_base(h)).astype(F32)
        groups.append(jnp.where(own_rows, pair_rows, ones_row))
    vt_ref[0] = jnp.concatenate(groups, axis=0).astype(BF16)


def _fox_prep_constants(nh, ts):
    G = nh * LANES
    tri = (lax.broadcasted_iota(I32, (ts, ts), 0) >= lax.broadcasted_iota(I32, (ts, ts), 1)).astype(BF16)
    sel = jnp.zeros((LANES, G), F32)
    mq = jnp.zeros((1, G), F32)
    cq = jnp.zeros((1, G), F32)
    ck = jnp.zeros((1, G), F32)
    for h in range(nh):
        base = h * LANES + _aux_base(h)
        for pc in range(N_PIECES):
            sel = sel.at[pc * nh + h, base + pc].set(1.0).at[pc * nh + h, base + N_PIECES + pc].set(-1.0)
            mq = mq.at[0, base + pc].set(1.0).at[0, base + N_PIECES + pc].set(-1.0)
            ck = ck.at[0, base + pc].set(1.0)
            cq = cq.at[0, base + N_PIECES + pc].set(1.0)
    return tri, sel.astype(BF16), mq, cq, ck


def _front_kernel(x_ref, mod_ref, ng_ref, w_ref, mu_ref, w0_ref, wup_ref, a0_ref, aup_ref, gup_ref,
                  fb_ref, tri_ref, sel_ref, mq_ref, cq_ref, ck_ref,
                  r_ref, k_ref, v_ref, wd_ref, a_ref, g_ref, qa_ref, ka_ref, vt_ref,
                  pr_s, q_s, kf_s, vf_s, f_s, carry_p, carry_f, *, n_rwkv, c_fox, c, nh, tm):
    @pl.when(pl.program_id(1) == 0)
    def _():
        carry_p[...] = jnp.zeros_like(carry_p)
        carry_f[...] = jnp.zeros_like(carry_f)

    for u in range(x_ref.shape[0] // tm):
        rows = pl.ds(u * tm, tm)
        s = u % 2
        _proj_kernel(x_ref.at[rows], mod_ref, ng_ref, w_ref, pr_s.at[s], q_s.at[s], kf_s.at[s], vf_s.at[s],
                     f_s.at[s], n_rwkv=n_rwkv, c_fox=c_fox)
        _rwkv_prep_kernel(pr_s.at[s], mu_ref, w0_ref, wup_ref, a0_ref, aup_ref, gup_ref,
                          r_ref.at[rows], k_ref.at[rows], v_ref.at[rows], wd_ref.at[rows], a_ref.at[rows],
                          g_ref.at[rows], carry_p, c=c)
        _fox_prep_kernel(q_s.at[s], kf_s.at[s], vf_s.at[s], f_s.at[s], fb_ref, tri_ref, sel_ref, mq_ref,
                         cq_ref, ck_ref, qa_ref.at[rows], ka_ref.at[rows], vt_ref.at[:, :, rows], carry_f, nh=nh)


FRONT_SUBTILES = 2


def _front(x2, mod3, norm_g, w_pad, mu, w0, wup, a0, aup, gup, fb_pad, B, S, n_rwkv, c, c_fox, nh, tm):
    T, D = x2.shape
    NW = w_pad.shape[1]
    G = nh * LANES
    tb = tm * FRONT_SUBTILES if S % (tm * FRONT_SUBTILES) == 0 else tm
    nt = S // tb
    tri, sel, mq, cq, ck = _fox_prep_constants(nh, tm)
    row = lambda b, i: (b * nt + i, 0)
    const = lambda b, i: (0, 0)
    const3 = lambda b, i: (0, 0, 0)
    tmaj = pl.BlockSpec((tb, c), lambda b, i: (i, b))
    return pl.pallas_call(
        functools.partial(_front_kernel, n_rwkv=n_rwkv, c_fox=c_fox, c=c, nh=nh, tm=tm),
        grid=(B, nt),
        in_specs=[pl.BlockSpec((tb, D), row), pl.BlockSpec((1, 6, D), lambda b, i: (b, 0, 0)),
                  pl.BlockSpec((1, D), const), pl.BlockSpec((D, NW), const),
                  pl.BlockSpec((1, n_rwkv), const), pl.BlockSpec((1, c), const),
                  pl.BlockSpec((2, LANES, c), const3), pl.BlockSpec((1, c), const),
                  pl.BlockSpec((2, LANES, c), const3), pl.BlockSpec((2, LANES, c), const3),
                  pl.BlockSpec((1, LANES), const), pl.BlockSpec((tm, tm), const),
                  pl.BlockSpec((LANES, G), const), pl.BlockSpec((1, G), const),
                  pl.BlockSpec((1, G), const), pl.BlockSpec((1, G), const)],
        out_specs=[tmaj] * 5 + [pl.BlockSpec((tb, c), row), pl.BlockSpec((tb, G), row),
                                pl.BlockSpec((tb, G), row), pl.BlockSpec((1, G, tb), lambda b, i: (b, 0, i))],
        out_shape=[jax.ShapeDtypeStruct((S, B * c), F32)] * 5
        + [jax.ShapeDtypeStruct((T, c), F32), jax.ShapeDtypeStruct((T, G), BF16),
           jax.ShapeDtypeStruct((T, G), BF16), jax.ShapeDtypeStruct((B, G, S), BF16)],
        scratch_shapes=[pltpu.VMEM((2, tm, n_rwkv), F32), pltpu.VMEM((2, tm, c_fox), BF16),
                        pltpu.VMEM((2, tm, c_fox), BF16), pltpu.VMEM((2, tm, c_fox), BF16),
                        pltpu.VMEM((2, tm, LANES), F32), pltpu.VMEM((1, n_rwkv), F32), pltpu.VMEM((1, LANES), F32)],
        compiler_params=_cparams(("arbitrary", "arbitrary")),
        name="front_proj_prep",
    )(x2, mod3, norm_g, w_pad, mu, w0, wup, a0, aup, gup, fb_pad, tri, sel, mq, cq, ck)


def _fox_kernel(q_ref, k_ref, vt_ref, o_ref, acc, mrow, s_a, s_b, lim, *, tq):
    qi = pl.program_id(2)
    tk = tq // 2
    n = 2 * (qi + 1)
    acc[...] = jnp.zeros_like(acc)
    mrow[...] = jnp.full_like(mrow, MASK_VALUE)

    @pl.when((pl.program_id(0) == 0) & (pl.program_id(1) == 0) & (qi == 0))
    def _():
        key = lax.broadcasted_iota(I32, (tk, tq), 0)
        query = lax.broadcasted_iota(I32, (tk, tq), 1)
        lim[0] = jnp.where(key <= query, -MASK_VALUE, MASK_VALUE)
        lim[1] = jnp.where(key + tk <= query, -MASK_VALUE, MASK_VALUE)

    def scores(kb, dst):
        off = pl.multiple_of(jnp.minimum(kb, n - 1) * tk, tk)
        for hh in range(2):
            grp = slice(hh * LANES, (hh + 1) * LANES)
            dst[hh] = lax.dot_general(k_ref[0, pl.ds(off, tk), grp], q_ref[0, :, grp],
                                      (((1,), (1,)), ((), ())), preferred_element_type=F32)

    def softmax_pv(kb, src, diag=None):
        off = pl.multiple_of(kb * tk, tk)
        for hh in range(2):
            grp = slice(hh * LANES, (hh + 1) * LANES)
            st = src[hh] if diag is None else jnp.minimum(src[hh], lim[diag])
            m_old = mrow[hh]
            m_new = jnp.maximum(m_old, jnp.max(st, axis=0, keepdims=True))
            mrow[hh] = m_new
            p = jnp.exp2(st - m_new).astype(BF16)
            pv = jnp.dot(vt_ref[0, grp, pl.ds(off, tk)], p, preferred_element_type=F32)
            acc[hh] = jnp.exp2(m_old - m_new) * acc[hh] + pv

    scores(0, s_a)

    def trip(j, c):
        scores(2 * j + 1, s_b)
        softmax_pv(2 * j, s_a)
        scores(2 * j + 2, s_a)
        softmax_pv(2 * j + 1, s_b)
        return c

    lax.fori_loop(0, qi, trip, 0)
    scores(n - 1, s_b)
    softmax_pv(n - 2, s_a, diag=0)
    softmax_pv(n - 1, s_b, diag=1)

    row = lax.broadcasted_iota(I32, (LANES, 1), 0)
    out_t = jnp.zeros((LANES, tq), F32)
    for hh in range(2):
        a = acc[hh]
        base = _aux_base(hh)
        l = a[base:base + 1, :]
        out_t = jnp.where((row // HEAD_DIM) == hh, a / l, out_t)
    o_ref[0] = out_t.T


def _fox_attention(qa, ka, vt, c_fox, tq):
    B, S, G = qa.shape
    nh = G // LANES
    return pl.pallas_call(
        functools.partial(_fox_kernel, tq=tq),
        grid=(B, nh // 2, S // tq),
        in_specs=[pl.BlockSpec((1, tq, 2 * LANES), lambda b, p, i: (b, i, p)),
                  pl.BlockSpec((1, S, 2 * LANES), lambda b, p, i: (b, 0, p)),
                  pl.BlockSpec((1, 2 * LANES, S), lambda b, p, i: (b, p, 0))],
        out_specs=pl.BlockSpec((1, tq, LANES), lambda b, p, i: (b, i, p)),
        out_shape=jax.ShapeDtypeStruct((B, S, c_fox), F32),
        scratch_shapes=[pltpu.VMEM((2, LANES, tq), F32), pltpu.VMEM((2, 1, tq), F32),
                        pltpu.VMEM((2, tq // 2, tq), F32), pltpu.VMEM((2, tq // 2, tq), F32),
                        pltpu.VMEM((2, tq // 2, tq), F32)],
        compiler_params=_cparams(("arbitrary", "arbitrary", "arbitrary")),
        name="fox_attention",
    )(qa, ka, vt)


def _mix_kernel(yr_ref, g_ref, yf_ref, x_ref, mod_ref, wo_ref, n2_ref, wr_ref,
                x1_ref, h2a_ref, h2b_ref, lg_ref, *, c):
    m = mod_ref[0]
    a = (yr_ref[...] * g_ref[...]).astype(BF16)
    b = yf_ref[...].astype(BF16)
    mix = (jnp.dot(a, wo_ref[0:c, :], preferred_element_type=F32)
           + jnp.dot(b, wo_ref[c:, :], preferred_element_type=F32))
    x1 = x_ref[...] + m[2:3] * mix
    x1_ref[...] = x1
    h2 = _rms(x1) * n2_ref[...] * (1.0 + m[4:5]) + m[3:4]
    words = _pack_bf16_halves(h2)
    hw = words.shape[1] // 2
    h2a_ref[...] = words[:, :hw]
    h2b_ref[...] = words[:, hw:]
    lg_ref[...] = _dot_split(h2, wr_ref)


def _pack_bf16_halves(x):
    w = x.shape[1] // 2
    bits = lax.bitcast_convert_type(x.astype(BF16).astype(F32), jnp.uint32)
    return (bits[:, :w] >> 16) | (bits[:, w:] & jnp.uint32(0xFFFF0000))


def _unpack_bf16_halves(words):
    lo = lax.bitcast_convert_type(words << 16, F32)
    hi = lax.bitcast_convert_type(words & jnp.uint32(0xFFFF0000), F32)
    return jnp.concatenate([lo, hi], axis=1).astype(BF16)


def _mix(y_rwkv, g, y_fox, x2, mod3, w_out, norm2_g, w_router, S, tm):
    T, D = x2.shape
    c = g.shape[1]
    nt = S // tm
    row = lambda i: (i, 0)
    const = lambda i: (0, 0)
    return pl.pallas_call(
        functools.partial(_mix_kernel, c=c),
        grid=(T // tm,),
        in_specs=[pl.BlockSpec((tm, c), lambda i: (i % nt, i // nt)),
                  pl.BlockSpec((tm, c), row), pl.BlockSpec((tm, c), row),
                  pl.BlockSpec((tm, D), row),
                  pl.BlockSpec((1, 6, D), lambda i: ((i * tm) // S, 0, 0)),
                  pl.BlockSpec((D, D), const), pl.BlockSpec((1, D), const),
                  pl.BlockSpec((2, D, LANES), lambda i: (0, 0, 0))],
        out_specs=[pl.BlockSpec((tm, D), row), pl.BlockSpec((tm, D // 4), row), pl.BlockSpec((tm, D // 4), row),
                   pl.BlockSpec((tm, LANES), row)],
        out_shape=[jax.ShapeDtypeStruct((T, D), F32), jax.ShapeDtypeStruct((T, D // 4), jnp.uint32),
                   jax.ShapeDtypeStruct((T, D // 4), jnp.uint32),
                   jax.ShapeDtypeStruct((T, LANES), F32)],
        compiler_params=_cparams(("arbitrary",)),
        name="out_proj_norm2_router",
    )(y_rwkv, g, y_fox, x2, mod3, w_out, norm2_g, w_router)


E_ROW0 = 8


def _first_argmax(vals, n):
    mx = jnp.max(vals, axis=0, keepdims=True)
    idx = lax.broadcasted_iota(I32, vals.shape, 0).astype(F32)
    first = jnp.min(jnp.where(vals == mx, idx, float(n)), axis=0, keepdims=True)
    return first.astype(I32), mx


def _route_kernel(lg_ref, bias_ref, tri_ref, ids_ref, wtok_ref, cnt_ref, carry):
    @pl.when(pl.program_id(0) == 0)
    def _():
        carry[...] = jnp.zeros_like(carry)

    lt = (lg_ref[...] + bias_ref[...]).T
    tm = lt.shape[1]
    grp = lt[0:N_GROUPS]
    ge = jnp.exp(grp - jnp.max(grp, axis=0, keepdims=True))
    gp = ge / jnp.sum(ge, axis=0, keepdims=True)
    g_sel, p_g = _first_argmax(gp, N_GROUPS)
    sel = jnp.zeros((EXPERTS_PER_GROUP, tm), F32)
    for g in range(N_GROUPS):
        lo = E_ROW0 + g * EXPERTS_PER_GROUP
        sel = jnp.where(g_sel == g, lt[lo:lo + EXPERTS_PER_GROUP], sel)
    ee = jnp.exp(sel - jnp.max(sel, axis=0, keepdims=True))
    ep = ee / jnp.sum(ee, axis=0, keepdims=True)
    i0, p0 = _first_argmax(ep, EXPERTS_PER_GROUP)
    idx8 = lax.broadcasted_iota(I32, ep.shape, 0)
    i1, p1 = _first_argmax(jnp.where(idx8 == i0, -1.0, ep), EXPERTS_PER_GROUP)
    den = p0 + p1
    w0 = p_g * p0 / den
    w1 = p_g * p1 / den
    e0 = g_sel * EXPERTS_PER_GROUP + i0
    e1 = g_sel * EXPERTS_PER_GROUP + i1

    ide = lax.broadcasted_iota(I32, (N_EXPERTS, tm), 0)
    oh0 = ide == e0
    oh1 = ide == e1
    oh = oh0.astype(F32) + oh1.astype(F32)
    incl = jnp.dot(oh.astype(BF16), tri_ref[...], preferred_element_type=F32)
    base = carry[...] + (incl - oh)
    r0 = jnp.sum(jnp.where(oh0, base, 0.0), axis=0, keepdims=True)
    r1 = jnp.sum(jnp.where(oh1, base, 0.0), axis=0, keepdims=True)
    carry[...] = carry[...] + incl[:, tm - 1:tm]
    cnt_ref[...] = jnp.broadcast_to(carry[...], cnt_ref.shape)
    ids_ref[...] = jnp.concatenate(
        [e0, e1, r0.astype(I32), r1.astype(I32), jnp.zeros((4, tm), I32)], axis=0)
    wtok_ref[...] = jnp.concatenate([w0, w1, jnp.zeros((LANES - 2, tm), F32)], axis=0).T


def _route(logits, bias_row, tm):
    T = logits.shape[0]
    tri = (lax.broadcasted_iota(I32, (tm, tm), 0) <= lax.broadcasted_iota(I32, (tm, tm), 1)).astype(BF16)
    return pl.pallas_call(
        _route_kernel,
        grid=(T // tm,),
        in_specs=[pl.BlockSpec((tm, LANES), lambda i: (i, 0)),
                  pl.BlockSpec((1, LANES), lambda i: (0, 0)),
                  pl.BlockSpec((tm, tm), lambda i: (0, 0))],
        out_specs=[pl.BlockSpec((8, tm), lambda i: (0, i)),
                   pl.BlockSpec((tm, LANES), lambda i: (i, 0)),
                   pl.BlockSpec((N_EXPERTS, LANES), lambda i: (0, 0))],
        out_shape=[jax.ShapeDtypeStruct((8, T), I32), jax.ShapeDtypeStruct((T, LANES), F32),
                   jax.ShapeDtypeStruct((N_EXPERTS, LANES), F32)],
        scratch_shapes=[pltpu.VMEM((N_EXPERTS, 1), F32)],
        compiler_params=_cparams(("arbitrary",)),
        name="route_rank",
    )(logits, bias_row, tri)


def _dest_kernel(ids_ref, ps_ref, d_ref):
    ids = ids_ref[...]
    tm = ids.shape[1]
    ide = lax.broadcasted_iota(I32, (N_EXPERTS, tm), 0)
    ps = ps_ref[...]
    rows = [jnp.sum(jnp.where(ide == ids[k:k + 1], ps, 0), axis=0, keepdims=True) + ids[2 + k:3 + k]
            for k in range(2)]
    d_ref[...] = jnp.concatenate(rows + [jnp.zeros((6, tm), I32)], axis=0)


def _dest_rows(ids, pstarts, tm):
    T = ids.shape[1]
    return pl.pallas_call(
        _dest_kernel,
        grid=(T // tm,),
        in_specs=[pl.BlockSpec((8, tm), lambda i: (0, i)), pl.BlockSpec((N_EXPERTS, 1), lambda i: (0, 0))],
        out_specs=pl.BlockSpec((8, tm), lambda i: (0, i)),
        out_shape=jax.ShapeDtypeStruct((8, T), I32),
        compiler_params=_cparams(("arbitrary",)),
        name="dest_rows",
    )(ids, pstarts.reshape(N_EXPERTS, 1))


Y_PARTS = 4


def _expert_kernel(be_ref, nu_ref, nv_ref, xa_ref, xb_ref, wg_ref, wu_ref, wd_ref, *rest):
    y_refs, (wg_s, wu_s, wd_s) = rest[:Y_PARTS], rest[Y_PARTS:]
    dq = y_refs[0].shape[1]
    i = pl.program_id(0)
    used = i < nu_ref[0]

    @pl.when(used & ((i == 0) | (be_ref[i] != be_ref[jnp.maximum(i - 1, 0)])))
    def _():
        wg_s[...] = wg_ref[0].astype(BF16)
        wu_s[...] = wu_ref[0].astype(BF16)
        wd_s[...] = wd_ref[0].astype(BF16)

    @pl.when(used)
    def _():
        x = _unpack_bf16_halves(jnp.concatenate([xa_ref[...], xb_ref[...]], axis=1))
        row = lax.broadcasted_iota(I32, (x.shape[0], 1), 0)
        x = jnp.where(row < nv_ref[i], x, jnp.zeros_like(x))
        g = jnp.dot(x, wg_s[...], preferred_element_type=F32)
        u = jnp.dot(x, wu_s[...], preferred_element_type=F32)
        hid = (g * _sigmoid(g) * u).astype(BF16)
        for q, y_ref in enumerate(y_refs):
            y_ref[...] = jnp.dot(hid, wd_s[:, q * dq:(q + 1) * dq], preferred_element_type=F32)

    @pl.when(jnp.logical_not(used))
    def _():
        for y_ref in y_refs:
            y_ref[...] = jnp.zeros_like(y_ref)


def _experts(blk_e, n_used, n_valid, xs_a, xs_b, wg, wu, wd, tme):
    P, W = xs_a.shape
    _, D, F = wg.shape
    grid_spec = pltpu.PrefetchScalarGridSpec(
        num_scalar_prefetch=3,
        grid=(P // tme,),
        in_specs=[pl.BlockSpec((tme, W), lambda i, *_: (i, 0)),
                  pl.BlockSpec((tme, W), lambda i, *_: (i, 0)),
                  pl.BlockSpec((1, D, F), lambda i, be, *_: (be[i], 0, 0)),
                  pl.BlockSpec((1, D, F), lambda i, be, *_: (be[i], 0, 0)),
                  pl.BlockSpec((1, F, D), lambda i, be, *_: (be[i], 0, 0))],
        out_specs=[pl.BlockSpec((tme, D // Y_PARTS), lambda i, *_: (i, 0))] * Y_PARTS,
        scratch_shapes=[pltpu.VMEM((D, F), BF16), pltpu.VMEM((D, F), BF16), pltpu.VMEM((F, D), BF16)],
    )
    return pl.pallas_call(
        _expert_kernel,
        grid_spec=grid_spec,
        out_shape=[jax.ShapeDtypeStruct((P, D // Y_PARTS), F32)] * Y_PARTS,
        compiler_params=_cparams(("arbitrary",)),
        name="moe_experts",
    )(blk_e, n_used, n_valid, xs_a, xs_b, wg, wu, wd)


SC_GATHER_WINDOW = 128


def _sc_row_gather(table, idx):
    M = idx.shape[0]
    W = table.shape[1]
    mesh = plsc.VectorSubcoreMesh(core_axis_name="c", subcore_axis_name="s")

    @functools.partial(pl.kernel, out_type=jax.ShapeDtypeStruct((M, W), table.dtype), mesh=mesh)
    def gather_kernel(x_hbm, i_hbm, o_hbm):
        def body(i_vmem, o_vmem):
            pltpu.sync_copy(x_hbm.at[i_vmem.at[0]], o_vmem)

        pltpu.emit_pipeline(
            body,
            grid=(M // SC_GATHER_WINDOW,),
            in_specs=[pl.BlockSpec((1, SC_GATHER_WINDOW), lambda i: (0, i))],
            out_specs=[pl.BlockSpec((SC_GATHER_WINDOW, W), lambda i: (i, 0))],
            core_axis_name=("c", "s"),
            dimension_semantics=(pltpu.PARALLEL,),
        )(i_hbm, o_hbm)

    return gather_kernel(table, idx.reshape(1, M))


def _sc_row_scatter(rows, idx, n_out):
    R, W = rows.shape
    M = idx.shape[0]
    nr = R // SC_GATHER_WINDOW
    mesh = plsc.VectorSubcoreMesh(core_axis_name="c", subcore_axis_name="s")

    @functools.partial(pl.kernel, out_type=jax.ShapeDtypeStruct((n_out, W), rows.dtype), mesh=mesh)
    def scatter_kernel(x_hbm, i_hbm, o_hbm):
        def body(x_vmem, i_vmem):
            pltpu.sync_copy(x_vmem, o_hbm.at[i_vmem.at[0]])

        pltpu.emit_pipeline(
            body,
            grid=(M // SC_GATHER_WINDOW,),
            in_specs=[pl.BlockSpec((SC_GATHER_WINDOW, W), lambda i: (i % nr, 0)),
                      pl.BlockSpec((1, SC_GATHER_WINDOW), lambda i: (0, i))],
            out_specs=[],
            core_axis_name=("c", "s"),
            dimension_semantics=(pltpu.PARALLEL,),
        )(x_hbm, i_hbm)

    return scatter_kernel(rows, idx.reshape(1, M))


def _combine_kernel(*refs):
    g_refs = refs[:2 * Y_PARTS]
    wtok_ref, x1_ref, mod_ref, gf_ref, o_ref = refs[2 * Y_PARTS:]
    m = mod_ref[0]
    w = wtok_ref[...]
    y0 = jnp.concatenate([r[...] for r in g_refs[:Y_PARTS]], axis=1)
    y1 = jnp.concatenate([r[...] for r in g_refs[Y_PARTS:]], axis=1)
    ff = w[:, 0:1] * y0 + w[:, 1:2] * y1
    x2 = x1_ref[...] + m[5:6] * ff
    o_ref[...] = _rms(x2) * gf_ref[...]


def _combine(gathered, wtok, x1, mod3, norm_f_g, S, tmc):
    T, D = x1.shape
    n = T // tmc
    dq = D // Y_PARTS
    g_specs = [pl.BlockSpec((tmc, dq), lambda i, k=k: (i + k * n, 0)) for k in range(2) for _ in range(Y_PARTS)]
    g_args = [gathered[q] for _ in range(2) for q in range(Y_PARTS)]
    return pl.pallas_call(
        _combine_kernel,
        grid=(n,),
        in_specs=g_specs + [pl.BlockSpec((tmc, LANES), lambda i: (i, 0)),
                            pl.BlockSpec((tmc, D), lambda i: (i, 0)),
                            pl.BlockSpec((1, 6, D), lambda i: ((i * tmc) // S, 0, 0)),
                            pl.BlockSpec((1, D), lambda i: (0, 0))],
        out_specs=pl.BlockSpec((tmc, D), lambda i: (i, 0)),
        out_shape=jax.ShapeDtypeStruct((T, D), F32),
        compiler_params=_cparams(("arbitrary",)),
        name="moe_combine_final_norm",
    )(*g_args, wtok, x1, mod3, norm_f_g)


def _pick(n, pref):
    t = min(pref, n)
    while n % t:
        t //= 2
    return t


def _layer(x, mod3, norm1_g, w_in, rwkv_mu, rwkv_w0, rwkv_w_up, rwkv_a0, rwkv_a_up, rwkv_g_up, rwkv_k_k,
           rwkv_k_a, rwkv_r_k, rwkv_lnx_g, rwkv_lnx_b, fox_f_bias, w_out, norm2_g, moe_w_grp, moe_b_grp,
           moe_w_rt, moe_b_rt, moe_w_gate, moe_w_up, moe_w_down):
    B, S, D = x.shape
    T = B * S
    c = rwkv_w0.shape[0]
    nh_r = c // HEAD_DIM
    n_rwkv = rwkv_mu.shape[0]
    nh_f = fox_f_bias.shape[0]
    c_fox = nh_f * HEAD_DIM
    d_lora = rwkv_w_up.shape[0]
    x2 = x.reshape(T, D)

    w_pad = jnp.pad(w_in, ((0, 0), (0, LANES - nh_f))).astype(BF16)
    wup_pad = jnp.pad(rwkv_w_up, ((0, LANES - d_lora), (0, 0)))
    aup_pad = jnp.pad(rwkv_a_up, ((d_lora, LANES - d_lora - rwkv_a_up.shape[0]), (0, 0)))
    fb_pad = jnp.pad(fox_f_bias, (0, LANES - nh_f)).reshape(1, LANES)
    r_t, k_t, v_t, w_t, a_t, g_t, qa, ka, vt = _front(
        x2, mod3, norm1_g.reshape(1, D), w_pad, rwkv_mu.reshape(1, n_rwkv), rwkv_w0.reshape(1, c),
        _split_bf16(wup_pad), rwkv_a0.reshape(1, c), _split_bf16(aup_pad), _split_bf16(rwkv_g_up), fb_pad,
        B, S, n_rwkv, c, c_fox, nh_f, _pick(S, 256))
    G = nh_f * LANES

    inst = B * nh_r
    to_scan = lambda t: t.reshape(S, inst, HEAD_DIM).transpose(0, 2, 1)
    per_inst = lambda p: jnp.tile(p.reshape(nh_r, HEAD_DIM).T, (1, B))
    scan_in = [to_scan(t) for t in (r_t, k_t, v_t, w_t, a_t)]
    y_scan = _wkv_scan(*scan_in, per_inst(rwkv_k_k), per_inst(rwkv_k_a), per_inst(rwkv_r_k),
                       per_inst(rwkv_lnx_g), per_inst(rwkv_lnx_b), _pick(S, 32))
    y_rwkv = y_scan.transpose(0, 2, 1).reshape(S, B * c)
    y_fox = _fox_attention(qa.reshape(B, S, G), ka.reshape(B, S, G), vt, c_fox, _pick(S, 512)).reshape(T, c_fox)

    w_router = jnp.zeros((D, LANES), F32)
    w_router = w_router.at[:, 0:N_GROUPS].set(moe_w_grp).at[:, E_ROW0:E_ROW0 + N_EXPERTS].set(moe_w_rt)
    b_router = jnp.zeros((1, LANES), F32)
    b_router = b_router.at[0, 0:N_GROUPS].set(moe_b_grp).at[0, E_ROW0:E_ROW0 + N_EXPERTS].set(moe_b_rt)
    x1, h2_a, h2_b, logits = _mix(y_rwkv, g_t, y_fox, x2, mod3, w_out.astype(BF16), norm2_g.reshape(1, D),
                          _split_bf16(w_router), S, _pick(S, 256))

    ids, wtok, cnt = _route(logits, b_router, _pick(T, 1024))
    counts = cnt[:, 0].astype(I32)
    padded = (counts + EXPERT_ROWS - 1) // EXPERT_ROWS * EXPERT_ROWS
    pends = jnp.cumsum(padded).astype(I32)
    pstarts = pends - padded
    n_rows = (2 * T + N_EXPERTS * (EXPERT_ROWS - 1) + EXPERT_ROWS - 1) // EXPERT_ROWS * EXPERT_ROWS
    n_blocks = n_rows // EXPERT_ROWS
    blk_start = jnp.arange(n_blocks, dtype=I32) * EXPERT_ROWS
    blk_e = jnp.sum((pends[None, :] <= blk_start[:, None]).astype(I32), axis=1)
    blk_e = jnp.minimum(blk_e, N_EXPERTS - 1)
    n_used = pends[-1:] // EXPERT_ROWS

    blk_first = blk_start - pstarts[blk_e]
    n_valid = jnp.clip(counts[blk_e] - blk_first, 0, EXPERT_ROWS).astype(I32)

    dest = _dest_rows(ids, pstarts, _pick(T, 1024))
    idx = jnp.concatenate([dest[0], dest[1]])
    xs_a = _sc_row_scatter(h2_a, idx, n_rows)
    xs_b = _sc_row_scatter(h2_b, idx, n_rows)
    ys_parts = _experts(blk_e, n_used, n_valid, xs_a, xs_b, moe_w_gate, moe_w_up, moe_w_down, EXPERT_ROWS)
    gathered = [_sc_row_gather(part, idx) for part in ys_parts]
    return x1, gathered, wtok


def kernel(x, c, w_ada, b_ada, norm1_g, w_in, rwkv_mu, rwkv_w0, rwkv_w_up, rwkv_a0, rwkv_a_up, rwkv_g_up, rwkv_k_k, rwkv_k_a, rwkv_r_k, rwkv_lnx_g, rwkv_lnx_b, fox_f_bias, w_out, norm2_g, moe_w_grp, moe_b_grp, moe_w_rt, moe_b_rt, moe_w_gate, moe_w_up, moe_w_down, norm_f_g):
    B, S, D = x.shape
    assert w_ada.shape[0] == 1, "single-layer model"
    mod3 = _adaln_mod(c, w_ada[0], b_ada[0]).reshape(B, 6, D)
    x1, gathered, wtok = _layer(
        x, mod3, norm1_g[0], w_in[0], rwkv_mu[0], rwkv_w0[0], rwkv_w_up[0], rwkv_a0[0], rwkv_a_up[0],
        rwkv_g_up[0], rwkv_k_k[0], rwkv_k_a[0], rwkv_r_k[0], rwkv_lnx_g[0], rwkv_lnx_b[0], fox_f_bias[0],
        w_out[0], norm2_g[0], moe_w_grp[0], moe_b_grp[0], moe_w_rt[0], moe_b_rt[0], moe_w_gate[0],
        moe_w_up[0], moe_w_down[0])
    out = _combine(gathered, wtok, x1, mod3, norm_f_g.reshape(1, D), S, _pick(B * S, 256))
    return out.reshape(B, S, D)
```

```python
import functools

import jax
import jax.numpy as jnp
from jax import lax
from jax.experimental import pallas as pl
from jax.experimental.pallas import tpu as pltpu
from jax.experimental.pallas import tpu_sc as plsc

F32 = jnp.float32
BF16 = jnp.bfloat16
I32 = jnp.int32
HIGHEST = lax.Precision.HIGHEST

HEAD_DIM = 64
N_GROUPS = 4
EXPERTS_PER_GROUP = 8
N_EXPERTS = N_GROUPS * EXPERTS_PER_GROUP
NORM_EPS = 1e-6
GN_EPS = 64e-5
LANES = 128
VMEM_LIMIT = 56 * 1024 * 1024

EXPERT_ROWS = 512
MASK_VALUE = -1e30


def _cparams(sem):
    return pltpu.CompilerParams(dimension_semantics=sem, vmem_limit_bytes=VMEM_LIMIT)


def _sigmoid(x):
    return 1.0 / (1.0 + jnp.exp(-x))


def _softplus(x):
    return jnp.maximum(x, 0.0) + jnp.log(1.0 + jnp.exp(-jnp.abs(x)))


def _split_bf16(w):
    hi = w.astype(BF16)
    return jnp.stack([hi, (w - hi.astype(F32)).astype(BF16)])


def _dot_split(a, w_ref):
    a_hi = a.astype(BF16)
    a_lo = (a - a_hi.astype(F32)).astype(BF16)
    w_hi = w_ref[0]
    return (jnp.dot(a_hi, w_hi, preferred_element_type=F32) + jnp.dot(a_lo, w_hi, preferred_element_type=F32)
            + jnp.dot(a_hi, w_ref[1], preferred_element_type=F32))


def _rms(x):
    return x * lax.rsqrt(jnp.mean(x * x, axis=-1, keepdims=True) + NORM_EPS)


def _mod_kernel(c_ref, w_ref, b_ref, o_ref):
    c = c_ref[...]
    cond = c * _sigmoid(c)
    o_ref[...] = jnp.dot(cond, w_ref[...], precision=HIGHEST, preferred_element_type=F32) + b_ref[...]


def _adaln_mod(c, w_ada, b_ada):
    B, D = c.shape
    N = w_ada.shape[1]
    tn = D
    return pl.pallas_call(
        _mod_kernel,
        grid=(N // tn,),
        in_specs=[pl.BlockSpec((B, D), lambda j: (0, 0)),
                  pl.BlockSpec((D, tn), lambda j: (0, j)),
                  pl.BlockSpec((1, tn), lambda j: (0, j))],
        out_specs=pl.BlockSpec((B, tn), lambda j: (0, j)),
        out_shape=jax.ShapeDtypeStruct((B, N), F32),
        compiler_params=_cparams(("arbitrary",)),
        name="adaln_mod",
    )(c, w_ada, b_ada.reshape(1, N))


def _proj_kernel(x_ref, mod_ref, g_ref, w_ref, pr_ref, q_ref, k_ref, v_ref, f_ref, *, n_rwkv, c_fox):
    m = mod_ref[0]
    h = (_rms(x_ref[...]) * g_ref[...] * (1.0 + m[1:2]) + m[0:1]).astype(BF16)
    o = n_rwkv
    pr_ref[...] = jnp.dot(h, w_ref[:, 0:o], preferred_element_type=F32)
    q = jnp.dot(h, w_ref[:, o:o + c_fox], preferred_element_type=F32)
    q_ref[...] = (q * (LOG2E * HEAD_DIM ** -0.5)).astype(BF16)
    k_ref[...] = jnp.dot(h, w_ref[:, o + c_fox:o + 2 * c_fox], preferred_element_type=F32).astype(BF16)
    v_ref[...] = jnp.dot(h, w_ref[:, o + 2 * c_fox:o + 3 * c_fox], preferred_element_type=F32).astype(BF16)
    f_ref[...] = jnp.dot(h, w_ref[:, o + 3 * c_fox:o + 3 * c_fox + LANES], preferred_element_type=F32)


def _rwkv_prep_kernel(p_ref, mu_ref, w0_ref, wup_ref, a0_ref, aup_ref, gup_ref,
                      r_ref, k_ref, v_ref, w_ref, a_ref, g_ref, carry, *, c):
    p = p_ref[...]
    tt = p.shape[0]
    row = lax.broadcasted_iota(I32, (tt, 1), 0)
    prev = jnp.where(row == 0, carry[...], pltpu.roll(p, 1, 0))
    carry[...] = p[tt - 1:tt, :]
    xs = p + (prev - p) * mu_ref[...]
    r_ref[...] = xs[:, 0:c]
    k_ref[...] = xs[:, c:2 * c]
    v_ref[...] = xs[:, 2 * c:3 * c]
    lo = xs[:, 3 * c:3 * c + LANES]
    wl = _dot_split(jnp.tanh(lo), wup_ref)
    w_log = -_softplus(-(w0_ref[...] + wl)) - 0.5
    w_ref[...] = jnp.exp(-jnp.exp(w_log))
    al = _dot_split(lo, aup_ref)
    a_ref[...] = _sigmoid(a0_ref[...] + al)
    gd = _sigmoid(xs[:, 3 * c + LANES:3 * c + 2 * LANES])
    g_ref[...] = _dot_split(gd, gup_ref)


ROW_UNROLL = 8


def _scan_kernel(r_ref, k_ref, v_ref, w_ref, a_ref, kkc_ref, kac_ref, rkc_ref, lg_ref, lb_ref,
                 y_ref, st, kk_s, b_s, km_s, rr_s, wprev_s, winc_s, yraw):
    @pl.when(pl.program_id(0) == 0)
    def _():
        st[...] = jnp.zeros_like(st)

    tc, n, lanes = r_ref.shape
    k = k_ref[...]
    a = a_ref[...]
    r = r_ref[...]
    kkr = k * kkc_ref[...][None]
    nrm = jnp.sqrt(jnp.sum(kkr * kkr, axis=1, keepdims=True))
    kk = kkr / jnp.maximum(nrm, 1e-12)
    km = k * (1.0 + (a - 1.0) * kac_ref[...][None])
    y_ref[...] = jnp.sum(r * km * rkc_ref[...][None], axis=1, keepdims=True) * v_ref[...]

    def cumulate(t, wc):
        wprev_s[t] = wc
        wc = wc * w_ref[t]
        winc_s[t] = wc
        return wc

    w_chunk = lax.fori_loop(0, tc, cumulate, jnp.ones((n, lanes), F32))
    winc = winc_s[...]
    inv = 1.0 / winc
    kk_s[...] = -(wprev_s[...] * kk)
    b_s[...] = kk * a * inv
    km_s[...] = km * inv
    rr_s[...] = r * winc

    def step(t, carry):
        kk_t = kk_s[t]
        b_t = b_s[t]
        k_t = km_s[t]
        r_t = rr_s[t]

        def rows(j, c2):
            for u in range(ROW_UNROLL):
                vv = j * ROW_UNROLL + u
                s_old = st[vv]
                sa = jnp.sum(s_old * kk_t, axis=0, keepdims=True)
                s_new = s_old + sa * b_t + v_ref[t, pl.ds(vv, 1), :] * k_t
                st[vv] = s_new
                yraw[t, pl.ds(vv, 1), :] = jnp.sum(s_new * r_t, axis=0, keepdims=True)
            return c2

        lax.fori_loop(0, n // ROW_UNROLL, rows, 0, unroll=True)
        return carry

    lax.fori_loop(0, tc, step, 0)
    st[...] = st[...] * w_chunk[None]

    y = yraw[...]
    mean = jnp.mean(y, axis=1, keepdims=True)
    yc = y - mean
    var = jnp.mean(yc * yc, axis=1, keepdims=True)
    yn = yc * lax.rsqrt(var + GN_EPS) * lg_ref[...][None] + lb_ref[...][None]
    y_ref[...] = y_ref[...] + yn


def _wkv_scan(r, k, v, w, a, kkc, kac, rkc, lg, lb, tc):
    S, n, L = r.shape
    blk = pl.BlockSpec((tc, n, L), lambda i: (i, 0, 0))
    cst = pl.BlockSpec((n, L), lambda i: (0, 0))
    return pl.pallas_call(
        _scan_kernel,
        grid=(S // tc,),
        in_specs=[blk] * 5 + [cst] * 5,
        out_specs=blk,
        out_shape=jax.ShapeDtypeStruct((S, n, L), F32),
        scratch_shapes=[pltpu.VMEM((n, n, L), F32)] + [pltpu.VMEM((tc, n, L), F32)] * 7,
        compiler_params=_cparams(("arbitrary",)),
        name="wkv_scan",
    )(r, k, v, w, a, kkc, kac, rkc, lg, lb)


LOG2E = 1.4426950408889634
N_PIECES = 3


def _aux_base(h):
    return (1 - h % 2) * HEAD_DIM


def _fox_prep_kernel(q_ref, k_ref, v_ref, f_ref, fb_ref, tri_ref, sel_ref, mq_ref, cq_ref, ck_ref,
                     qa_ref, ka_ref, vt_ref, carry, *, nh):
    lane1 = lax.broadcasted_iota(I32, (1, LANES), 1)

    def pack_pieces(x):
        pieces, rest = [], x
        for _ in range(N_PIECES):
            piece = rest.astype(BF16).astype(F32)
            pieces.append(piece)
            rest = rest - piece
        packed = jnp.where(lane1 < nh, pieces[0],
                           jnp.where(lane1 < 2 * nh, pltpu.roll(pieces[1], nh, 1),
                                     jnp.where(lane1 < 3 * nh, pltpu.roll(pieces[2], 2 * nh, 1), 0.0)))
        return packed.astype(BF16)

    z = f_ref[...] + fb_ref[...]
    lf = -_softplus(-z)
    part = jnp.dot(tri_ref[...], pack_pieces(lf), preferred_element_type=F32)
    inc = part + pltpu.roll(part, LANES - nh, 1) + pltpu.roll(part, LANES - 2 * nh, 1) + carry[...]
    ts = inc.shape[0]
    carry[...] = inc[ts - 1:ts, :]

    aux = jnp.dot(pack_pieces(inc * LOG2E), sel_ref[...], preferred_element_type=F32)
    aux_q = jnp.where(mq_ref[...] > 0, aux, cq_ref[...])
    aux_k = jnp.where(mq_ref[...] < 0, aux, ck_ref[...])

    lane = lax.broadcasted_iota(I32, (1, nh * LANES), 1)
    own = ((lane // HEAD_DIM) % 2) == ((lane // LANES) % 2)
    q = q_ref[...]
    k = k_ref[...]
    dup = lambda t: jnp.concatenate(
        [t[:, (h // 2) * LANES:(h // 2 + 1) * LANES] for h in range(nh)], axis=1)
    qa_ref[...] = jnp.where(own, dup(q), aux_q.astype(BF16))
    ka_ref[...] = jnp.where(own, dup(k), aux_k.astype(BF16))

    vt = v_ref[...].astype(F32).T
    row = lax.broadcasted_iota(I32, (LANES, 1), 0)
    groups = []
    for h in range(nh):
        pair_rows = vt[(h // 2) * LANES:(h // 2 + 1) * LANES]
        own_rows = (row // HEAD_DIM) == (h % 2)
        ones_row = (row == _aux_base(h)).astype(F32)
        groups.append(jnp.where(own_rows, pair_rows, ones_row))
    vt_ref[0] = jnp.concatenate(groups, axis=0).astype(BF16)


def _fox_prep_constants(nh, ts):
    G = nh * LANES
    tri = (lax.broadcasted_iota(I32, (ts, ts), 0) >= lax.broadcasted_iota(I32, (ts, ts), 1)).astype(BF16)
    sel = jnp.zeros((LANES, G), F32)
    mq = jnp.zeros((1, G), F32)
    cq = jnp.zeros((1, G), F32)
    ck = jnp.zeros((1, G), F32)
    for h in range(nh):
        base = h * LANES + _aux_base(h)
        for pc in range(N_PIECES):
            sel = sel.at[pc * nh + h, base + pc].set(1.0).at[pc * nh + h, base + N_PIECES + pc].set(-1.0)
            mq = mq.at[0, base + pc].set(1.0).at[0, base + N_PIECES + pc].set(-1.0)
            ck = ck.at[0, base + pc].set(1.0)
            cq = cq.at[0, base + N_PIECES + pc].set(1.0)
    return tri, sel.astype(BF16), mq, cq, ck


def _front_kernel(x_ref, mod_ref, ng_ref, w_ref, mu_ref, w0_ref, wup_ref, a0_ref, aup_ref, gup_ref,
                  fb_ref, tri_ref, sel_ref, mq_ref, cq_ref, ck_ref,
                  r_ref, k_ref, v_ref, wd_ref, a_ref, g_ref, qa_ref, ka_ref, vt_ref,
                  pr_s, q_s, kf_s, vf_s, f_s, carry_p, carry_f, *, n_rwkv, c_fox, c, nh, tm):
    @pl.when(pl.program_id(1) == 0)
    def _():
        carry_p[...] = jnp.zeros_like(carry_p)
        carry_f[...] = jnp.zeros_like(carry_f)

    for u in range(x_ref.shape[0] // tm):
        rows = pl.ds(u * tm, tm)
        s = u % 2
        _proj_kernel(x_ref.at[rows], mod_ref, ng_ref, w_ref, pr_s.at[s], q_s.at[s], kf_s.at[s], vf_s.at[s],
                     f_s.at[s], n_rwkv=n_rwkv, c_fox=c_fox)
        _rwkv_prep_kernel(pr_s.at[s], mu_ref, w0_ref, wup_ref, a0_ref, aup_ref, gup_ref,
                          r_ref.at[rows], k_ref.at[rows], v_ref.at[rows], wd_ref.at[rows], a_ref.at[rows],
                          g_ref.at[rows], carry_p, c=c)
        _fox_prep_kernel(q_s.at[s], kf_s.at[s], vf_s.at[s], f_s.at[s], fb_ref, tri_ref, sel_ref, mq_ref,
                         cq_ref, ck_ref, qa_ref.at[rows], ka_ref.at[rows], vt_ref.at[:, :, rows], carry_f, nh=nh)


FRONT_SUBTILES = 2


def _front(x2, mod3, norm_g, w_pad, mu, w0, wup, a0, aup, gup, fb_pad, B, S, n_rwkv, c, c_fox, nh, tm):
    T, D = x2.shape
    NW = w_pad.shape[1]
    G = nh * LANES
    tb = tm * FRONT_SUBTILES if S % (tm * FRONT_SUBTILES) == 0 else tm
    nt = S // tb
    tri, sel, mq, cq, ck = _fox_prep_constants(nh, tm)
    row = lambda b, i: (b * nt + i, 0)
    const = lambda b, i: (0, 0)
    const3 = lambda b, i: (0, 0, 0)
    tmaj = pl.BlockSpec((tb, c), lambda b, i: (i, b))
    return pl.pallas_call(
        functools.partial(_front_kernel, n_rwkv=n_rwkv, c_fox=c_fox, c=c, nh=nh, tm=tm),
        grid=(B, nt),
        in_specs=[pl.BlockSpec((tb, D), row), pl.BlockSpec((1, 6, D), lambda b, i: (b, 0, 0)),
                  pl.BlockSpec((1, D), const), pl.BlockSpec((D, NW), const),
                  pl.BlockSpec((1, n_rwkv), const), pl.BlockSpec((1, c), const),
                  pl.BlockSpec((2, LANES, c), const3), pl.BlockSpec((1, c), const),
                  pl.BlockSpec((2, LANES, c), const3), pl.BlockSpec((2, LANES, c), const3),
                  pl.BlockSpec((1, LANES), const), pl.BlockSpec((tm, tm), const),
                  pl.BlockSpec((LANES, G), const), pl.BlockSpec((1, G), const),
                  pl.BlockSpec((1, G), const), pl.BlockSpec((1, G), const)],
        out_specs=[tmaj] * 5 + [pl.BlockSpec((tb, c), row), pl.BlockSpec((tb, G), row),
                                pl.BlockSpec((tb, G), row), pl.BlockSpec((1, G, tb), lambda b, i: (b, 0, i))],
        out_shape=[jax.ShapeDtypeStruct((S, B * c), F32)] * 5
        + [jax.ShapeDtypeStruct((T, c), F32), jax.ShapeDtypeStruct((T, G), BF16),
           jax.ShapeDtypeStruct((T, G), BF16), jax.ShapeDtypeStruct((B, G, S), BF16)],
        scratch_shapes=[pltpu.VMEM((2, tm, n_rwkv), F32), pltpu.VMEM((2, tm, c_fox), BF16),
                        pltpu.VMEM((2, tm, c_fox), BF16), pltpu.VMEM((2, tm, c_fox), BF16),
                        pltpu.VMEM((2, tm, LANES), F32), pltpu.VMEM((1, n_rwkv), F32), pltpu.VMEM((1, LANES), F32)],
        compiler_params=_cparams(("arbitrary", "arbitrary")),
        name="front_proj_prep",
    )(x2, mod3, norm_g, w_pad, mu, w0, wup, a0, aup, gup, fb_pad, tri, sel, mq, cq, ck)


def _fox_kernel(q_ref, k_ref, vt_ref, o_ref, acc, mrow, s_a, s_b, lim, *, tq):
    qi = pl.program_id(2)
    tk = tq // 2
    n = 2 * (qi + 1)
    acc[...] = jnp.zeros_like(acc)
    mrow[...] = jnp.full_like(mrow, MASK_VALUE)

    @pl.when((pl.program_id(0) == 0) & (pl.program_id(1) == 0) & (qi == 0))
    def _():
        key = lax.broadcasted_iota(I32, (tk, tq), 0)
        query = lax.broadcasted_iota(I32, (tk, tq), 1)
        lim[0] = jnp.where(key <= query, -MASK_VALUE, MASK_VALUE)
        lim[1] = jnp.where(key + tk <= query, -MASK_VALUE, MASK_VALUE)

    def scores(kb, dst):
        off = pl.multiple_of(jnp.minimum(kb, n - 1) * tk, tk)
        for hh in range(2):
            grp = slice(hh * LANES, (hh + 1) * LANES)
            dst[hh] = lax.dot_general(k_ref[0, pl.ds(off, tk), grp], q_ref[0, :, grp],
                                      (((1,), (1,)), ((), ())), preferred_element_type=F32)

    def softmax_pv(kb, src, diag=None):
        off = pl.multiple_of(kb * tk, tk)
        for hh in range(2):
            grp = slice(hh * LANES, (hh + 1) * LANES)
            st = src[hh] if diag is None else jnp.minimum(src[hh], lim[diag])
            m_old = mrow[hh]
            m_new = jnp.maximum(m_old, jnp.max(st, axis=0, keepdims=True))
            mrow[hh] = m_new
            p = jnp.exp2(st - m_new).astype(BF16)
            pv = jnp.dot(vt_ref[0, grp, pl.ds(off, tk)], p, preferred_element_type=F32)
            acc[hh] = jnp.exp2(m_old - m_new) * acc[hh] + pv

    scores(0, s_a)

    def trip(j, c):
        scores(2 * j + 1, s_b)
        softmax_pv(2 * j, s_a)
        scores(2 * j + 2, s_a)
        softmax_pv(2 * j + 1, s_b)
        return c

    lax.fori_loop(0, qi, trip, 0)
    scores(n - 1, s_b)
    softmax_pv(n - 2, s_a, diag=0)
    softmax_pv(n - 1, s_b, diag=1)

    row = lax.broadcasted_iota(I32, (LANES, 1), 0)
    out_t = jnp.zeros((LANES, tq), F32)
    for hh in range(2):
        a = acc[hh]
        base = _aux_base(hh)
        l = a[base:base + 1, :]
        out_t = jnp.where((row // HEAD_DIM) == hh, a / l, out_t)
    o_ref[0] = out_t.T


def _fox_attention(qa, ka, vt, c_fox, tq):
    B, S, G = qa.shape
    nh = G // LANES
    return pl.pallas_call(
        functools.partial(_fox_kernel, tq=tq),
        grid=(B, nh // 2, S // tq),
        in_specs=[pl.BlockSpec((1, tq, 2 * LANES), lambda b, p, i: (b, i, p)),
                  pl.BlockSpec((1, S, 2 * LANES), lambda b, p, i: (b, 0, p)),
                  pl.BlockSpec((1, 2 * LANES, S), lambda b, p, i: (b, p, 0))],
        out_specs=pl.BlockSpec((1, tq, LANES), lambda b, p, i: (b, i, p)),
        out_shape=jax.ShapeDtypeStruct((B, S, c_fox), F32),
        scratch_shapes=[pltpu.VMEM((2, LANES, tq), F32), pltpu.VMEM((2, 1, tq), F32),
                        pltpu.VMEM((2, tq // 2, tq), F32), pltpu.VMEM((2, tq // 2, tq), F32),
                        pltpu.VMEM((2, tq // 2, tq), F32)],
        compiler_params=_cparams(("arbitrary", "arbitrary", "arbitrary")),
        name="fox_attention",
    )(qa, ka, vt)


def _mix_kernel(yr_ref, g_ref, yf_ref, x_ref, mod_ref, wo_ref, n2_ref, wr_ref,
                x1_ref, h2a_ref, h2b_ref, lg_ref, *, c):
    m = mod_ref[0]
    a = (yr_ref[...] * g_ref[...]).astype(BF16)
    b = yf_ref[...].astype(BF16)
    mix = (jnp.dot(a, wo_ref[0:c, :], preferred_element_type=F32)
           + jnp.dot(b, wo_ref[c:, :], preferred_element_type=F32))
    x1 = x_ref[...] + m[2:3] * mix
    x1_ref[...] = x1
    h2 = _rms(x1) * n2_ref[...] * (1.0 + m[4:5]) + m[3:4]
    words = _pack_bf16_halves(h2)
    hw = words.shape[1] // 2
    h2a_ref[...] = words[:, :hw]
    h2b_ref[...] = words[:, hw:]
    lg_ref[...] = _dot_split(h2, wr_ref)


def _pack_bf16_halves(x):
    w = x.shape[1] // 2
    bits = lax.bitcast_convert_type(x.astype(BF16).astype(F32), jnp.uint32)
    return (bits[:, :w] >> 16) | (bits[:, w:] & jnp.uint32(0xFFFF0000))


def _unpack_bf16_halves(words):
    lo = lax.bitcast_convert_type(words << 16, F32)
    hi = lax.bitcast_convert_type(words & jnp.uint32(0xFFFF0000), F32)
    return jnp.concatenate([lo, hi], axis=1).astype(BF16)


def _mix(y_rwkv, g, y_fox, x2, mod3, w_out, norm2_g, w_router, S, tm):
    T, D = x2.shape
    c = g.shape[1]
    nt = S // tm
    row = lambda i: (i, 0)
    const = lambda i: (0, 0)
    return pl.pallas_call(
        functools.partial(_mix_kernel, c=c),
        grid=(T // tm,),
        in_specs=[pl.BlockSpec((tm, c), lambda i: (i % nt, i // nt)),
                  pl.BlockSpec((tm, c), row), pl.BlockSpec((tm, c), row),
                  pl.BlockSpec((tm, D), row),
                  pl.BlockSpec((1, 6, D), lambda i: ((i * tm) // S, 0, 0)),
                  pl.BlockSpec((D, D), const), pl.BlockSpec((1, D), const),
                  pl.BlockSpec((2, D, LANES), lambda i: (0, 0, 0))],
        out_specs=[pl.BlockSpec((tm, D), row), pl.BlockSpec((tm, D // 4), row), pl.BlockSpec((tm, D // 4), row),
                   pl.BlockSpec((tm, LANES), row)],
        out_shape=[jax.ShapeDtypeStruct((T, D), F32), jax.ShapeDtypeStruct((T, D // 4), jnp.uint32),
                   jax.ShapeDtypeStruct((T, D // 4), jnp.uint32),
                   jax.ShapeDtypeStruct((T, LANES), F32)],
        compiler_params=_cparams(("arbitrary",)),
        name="out_proj_norm2_router",
    )(y_rwkv, g, y_fox, x2, mod3, w_out, norm2_g, w_router)


E_ROW0 = 8


def _first_argmax(vals, n):
    mx = jnp.max(vals, axis=0, keepdims=True)
    idx = lax.broadcasted_iota(I32, vals.shape, 0).astype(F32)
    first = jnp.min(jnp.where(vals == mx, idx, float(n)), axis=0, keepdims=True)
    return first.astype(I32), mx


def _route_kernel(lg_ref, bias_ref, tri_ref, ids_ref, wtok_ref, cnt_ref, carry):
    @pl.when(pl.program_id(0) == 0)
    def _():
        carry[...] = jnp.zeros_like(carry)

    lt = (lg_ref[...] + bias_ref[...]).T
    tm = lt.shape[1]
    grp = lt[0:N_GROUPS]
    ge = jnp.exp(grp - jnp.max(grp, axis=0, keepdims=True))
    gp = ge / jnp.sum(ge, axis=0, keepdims=True)
    g_sel, p_g = _first_argmax(gp, N_GROUPS)
    sel = jnp.zeros((EXPERTS_PER_GROUP, tm), F32)
    for g in range(N_GROUPS):
        lo = E_ROW0 + g * EXPERTS_PER_GROUP
        sel = jnp.where(g_sel == g, lt[lo:lo + EXPERTS_PER_GROUP], sel)
    ee = jnp.exp(sel - jnp.max(sel, axis=0, keepdims=True))
    ep = ee / jnp.sum(ee, axis=0, keepdims=True)
    i0, p0 = _first_argmax(ep, EXPERTS_PER_GROUP)
    idx8 = lax.broadcasted_iota(I32, ep.shape, 0)
    i1, p1 = _first_argmax(jnp.where(idx8 == i0, -1.0, ep), EXPERTS_PER_GROUP)
    den = p0 + p1
    w0 = p_g * p0 / den
    w1 = p_g * p1 / den
    e0 = g_sel * EXPERTS_PER_GROUP + i0
    e1 = g_sel * EXPERTS_PER_GROUP + i1

    ide = lax.broadcasted_iota(I32, (N_EXPERTS, tm), 0)
    oh0 = ide == e0
    oh1 = ide == e1
    oh = oh0.astype(F32) + oh1.astype(F32)
    incl = jnp.dot(oh.astype(BF16), tri_ref[...], preferred_element_type=F32)
    base = carry[...] + (incl - oh)
    r0 = jnp.sum(jnp.where(oh0, base, 0.0), axis=0, keepdims=True)
    r1 = jnp.sum(jnp.where(oh1, base, 0.0), axis=0, keepdims=True)
    carry[...] = carry[...] + incl[:, tm - 1:tm]
    cnt_ref[...] = jnp.broadcast_to(carry[...], cnt_ref.shape)
    ids_ref[...] = jnp.concatenate(
        [e0, e1, r0.astype(I32), r1.astype(I32), jnp.zeros((4, tm), I32)], axis=0)
    wtok_ref[...] = jnp.concatenate([w0, w1, jnp.zeros((LANES - 2, tm), F32)], axis=0).T


def _route(logits, bias_row, tm):
    T = logits.shape[0]
    tri = (lax.broadcasted_iota(I32, (tm, tm), 0) <= lax.broadcasted_iota(I32, (tm, tm), 1)).astype(BF16)
    return pl.pallas_call(
        _route_kernel,
        grid=(T // tm,),
        in_specs=[pl.BlockSpec((tm, LANES), lambda i: (i, 0)),
                  pl.BlockSpec((1, LANES), lambda i: (0, 0)),
                  pl.BlockSpec((tm, tm), lambda i: (0, 0))],
        out_specs=[pl.BlockSpec((8, tm), lambda i: (0, i)),
                   pl.BlockSpec((tm, LANES), lambda i: (i, 0)),
                   pl.BlockSpec((N_EXPERTS, LANES), lambda i: (0, 0))],
        out_shape=[jax.ShapeDtypeStruct((8, T), I32), jax.ShapeDtypeStruct((T, LANES), F32),
                   jax.ShapeDtypeStruct((N_EXPERTS, LANES), F32)],
        scratch_shapes=[pltpu.VMEM((N_EXPERTS, 1), F32)],
        compiler_params=_cparams(("arbitrary",)),
        name="route_rank",
    )(logits, bias_row, tri)


def _dest_kernel(ids_ref, ps_ref, d_ref):
    ids = ids_ref[...]
    tm = ids.shape[1]
    ide = lax.broadcasted_iota(I32, (N_EXPERTS, tm), 0)
    ps = ps_ref[...]
    rows = [jnp.sum(jnp.where(ide == ids[k:k + 1], ps, 0), axis=0, keepdims=True) + ids[2 + k:3 + k]
            for k in range(2)]
    d_ref[...] = jnp.concatenate(rows + [jnp.zeros((6, tm), I32)], axis=0)


def _dest_rows(ids, pstarts, tm):
    T = ids.shape[1]
    return pl.pallas_call(
        _dest_kernel,
        grid=(T // tm,),
        in_specs=[pl.BlockSpec((8, tm), lambda i: (0, i)), pl.BlockSpec((N_EXPERTS, 1), lambda i: (0, 0))],
        out_specs=pl.BlockSpec((8, tm), lambda i: (0, i)),
        out_shape=jax.ShapeDtypeStruct((8, T), I32),
        compiler_params=_cparams(("arbitrary",)),
        name="dest_rows",
    )(ids, pstarts.reshape(N_EXPERTS, 1))


Y_PARTS = 4


def _expert_kernel(be_ref, nu_ref, nv_ref, xa_ref, xb_ref, wg_ref, wu_ref, wd_ref, *rest):
    y_refs, (wg_s, wu_s, wd_s) = rest[:Y_PARTS], rest[Y_PARTS:]
    dq = y_refs[0].shape[1]
    i = pl.program_id(0)
    used = i < nu_ref[0]

    @pl.when(used & ((i == 0) | (be_ref[i] != be_ref[jnp.maximum(i - 1, 0)])))
    def _():
        wg_s[...] = wg_ref[0].astype(BF16)
        wu_s[...] = wu_ref[0].astype(BF16)
        wd_s[...] = wd_ref[0].astype(BF16)

    @pl.when(used)
    def _():
        x = _unpack_bf16_halves(jnp.concatenate([xa_ref[...], xb_ref[...]], axis=1))
        row = lax.broadcasted_iota(I32, (x.shape[0], 1), 0)
        x = jnp.where(row < nv_ref[i], x, jnp.zeros_like(x))
        g = jnp.dot(x, wg_s[...], preferred_element_type=F32)
        u = jnp.dot(x, wu_s[...], preferred_element_type=F32)
        hid = (g * _sigmoid(g) * u).astype(BF16)
        for q, y_ref in enumerate(y_refs):
            y_ref[...] = jnp.dot(hid, wd_s[:, q * dq:(q + 1) * dq], preferred_element_type=F32)

    @pl.when(jnp.logical_not(used))
    def _():
        for y_ref in y_refs:
            y_ref[...] = jnp.zeros_like(y_ref)


def _experts(blk_e, n_used, n_valid, xs_a, xs_b, wg, wu, wd, tme):
    P, W = xs_a.shape
    _, D, F = wg.shape
    grid_spec = pltpu.PrefetchScalarGridSpec(
        num_scalar_prefetch=3,
        grid=(P // tme,),
        in_specs=[pl.BlockSpec((tme, W), lambda i, *_: (i, 0)),
                  pl.BlockSpec((tme, W), lambda i, *_: (i, 0)),
                  pl.BlockSpec((1, D, F), lambda i, be, *_: (be[i], 0, 0)),
                  pl.BlockSpec((1, D, F), lambda i, be, *_: (be[i], 0, 0)),
                  pl.BlockSpec((1, F, D), lambda i, be, *_: (be[i], 0, 0))],
        out_specs=[pl.BlockSpec((tme, D // Y_PARTS), lambda i, *_: (i, 0))] * Y_PARTS,
        scratch_shapes=[pltpu.VMEM((D, F), BF16), pltpu.VMEM((D, F), BF16), pltpu.VMEM((F, D), BF16)],
    )
    return pl.pallas_call(
        _expert_kernel,
        grid_spec=grid_spec,
        out_shape=[jax.ShapeDtypeStruct((P, D // Y_PARTS), F32)] * Y_PARTS,
        compiler_params=_cparams(("arbitrary",)),
        name="moe_experts",
    )(blk_e, n_used, n_valid, xs_a, xs_b, wg, wu, wd)


SC_GATHER_WINDOW = 128


def _sc_row_gather(table, idx):
    M = idx.shape[0]
    W = table.shape[1]
    mesh = plsc.VectorSubcoreMesh(core_axis_name="c", subcore_axis_name="s")

    @functools.partial(pl.kernel, out_type=jax.ShapeDtypeStruct((M, W), table.dtype), mesh=mesh)
    def gather_kernel(x_hbm, i_hbm, o_hbm):
        def body(i_vmem, o_vmem):
            pltpu.sync_copy(x_hbm.at[i_vmem.at[0]], o_vmem)

        pltpu.emit_pipeline(
            body,
            grid=(M // SC_GATHER_WINDOW,),
            in_specs=[pl.BlockSpec((1, SC_GATHER_WINDOW), lambda i: (0, i))],
            out_specs=[pl.BlockSpec((SC_GATHER_WINDOW, W), lambda i: (i, 0))],
            core_axis_name=("c", "s"),
            dimension_semantics=(pltpu.PARALLEL,),
        )(i_hbm, o_hbm)

    return gather_kernel(table, idx.reshape(1, M))


def _sc_row_scatter(rows, idx, n_out):
    R, W = rows.shape
    M = idx.shape[0]
    nr = R // SC_GATHER_WINDOW
    mesh = plsc.VectorSubcoreMesh(core_axis_name="c", subcore_axis_name="s")

    @functools.partial(pl.kernel, out_type=jax.ShapeDtypeStruct((n_out, W), rows.dtype), mesh=mesh)
    def scatter_kernel(x_hbm, i_hbm, o_hbm):
        def body(x_vmem, i_vmem):
            pltpu.sync_copy(x_vmem, o_hbm.at[i_vmem.at[0]])

        pltpu.emit_pipeline(
            body,
            grid=(M // SC_GATHER_WINDOW,),
            in_specs=[pl.BlockSpec((SC_GATHER_WINDOW, W), lambda i: (i % nr, 0)),
                      pl.BlockSpec((1, SC_GATHER_WINDOW), lambda i: (0, i))],
            out_specs=[],
            core_axis_name=("c", "s"),
            dimension_semantics=(pltpu.PARALLEL,),
        )(x_hbm, i_hbm)

    return scatter_kernel(rows, idx.reshape(1, M))


def _combine_kernel(*refs):
    g_refs = refs[:2 * Y_PARTS]
    wtok_ref, x1_ref, mod_ref, gf_ref, o_ref = refs[2 * Y_PARTS:]
    m = mod_ref[0]
    w = wtok_ref[...]
    y0 = jnp.concatenate([r[...] for r in g_refs[:Y_PARTS]], axis=1)
    y1 = jnp.concatenate([r[...] for r in g_refs[Y_PARTS:]], axis=1)
    ff = w[:, 0:1] * y0 + w[:, 1:2] * y1
    x2 = x1_ref[...] + m[5:6] * ff
    o_ref[...] = _rms(x2) * gf_ref[...]


def _combine(gathered, wtok, x1, mod3, norm_f_g, S, tmc):
    T, D = x1.shape
    n = T // tmc
    dq = D // Y_PARTS
    g_specs = [pl.BlockSpec((tmc, dq), lambda i, k=k: (i + k * n, 0)) for k in range(2) for _ in range(Y_PARTS)]
    g_args = [gathered[q] for _ in range(2) for q in range(Y_PARTS)]
    return pl.pallas_call(
        _combine_kernel,
        grid=(n,),
        in_specs=g_specs + [pl.BlockSpec((tmc, LANES), lambda i: (i, 0)),
                            pl.BlockSpec((tmc, D), lambda i: (i, 0)),
                            pl.BlockSpec((1, 6, D), lambda i: ((i * tmc) // S, 0, 0)),
                            pl.BlockSpec((1, D), lambda i: (0, 0))],
        out_specs=pl.BlockSpec((tmc, D), lambda i: (i, 0)),
        out_shape=jax.ShapeDtypeStruct((T, D), F32),
        compiler_params=_cparams(("arbitrary",)),
        name="moe_combine_final_norm",
    )(*g_args, wtok, x1, mod3, norm_f_g)


def _pick(n, pref):
    t = min(pref, n)
    while n % t:
        t //= 2
    return t


def _layer(x, mod3, norm1_g, w_in, rwkv_mu, rwkv_w0, rwkv_w_up, rwkv_a0, rwkv_a_up, rwkv_g_up, rwkv_k_k,
           rwkv_k_a, rwkv_r_k, rwkv_lnx_g, rwkv_lnx_b, fox_f_bias, w_out, norm2_g, moe_w_grp, moe_b_grp,
           moe_w_rt, moe_b_rt, moe_w_gate, moe_w_up, moe_w_down):
    B, S, D = x.shape
    T = B * S
    c = rwkv_w0.shape[0]
    nh_r = c // HEAD_DIM
    n_rwkv = rwkv_mu.shape[0]
    nh_f = fox_f_bias.shape[0]
    c_fox = nh_f * HEAD_DIM
    d_lora = rwkv_w_up.shape[0]
    x2 = x.reshape(T, D)

    w_pad = jnp.pad(w_in, ((0, 0), (0, LANES - nh_f))).astype(BF16)
    wup_pad = jnp.pad(rwkv_w_up, ((0, LANES - d_lora), (0, 0)))
    aup_pad = jnp.pad(rwkv_a_up, ((d_lora, LANES - d_lora - rwkv_a_up.shape[0]), (0, 0)))
    fb_pad = jnp.pad(fox_f_bias, (0, LANES - nh_f)).reshape(1, LANES)
    r_t, k_t, v_t, w_t, a_t, g_t, qa, ka, vt = _front(
        x2, mod3, norm1_g.reshape(1, D), w_pad, rwkv_mu.reshape(1, n_rwkv), rwkv_w0.reshape(1, c),
        _split_bf16(wup_pad), rwkv_a0.reshape(1, c), _split_bf16(aup_pad), _split_bf16(rwkv_g_up), fb_pad,
        B, S, n_rwkv, c, c_fox, nh_f, _pick(S, 256))
    G = nh_f * LANES

    inst = B * nh_r
    to_scan = lambda t: t.reshape(S, inst, HEAD_DIM).transpose(0, 2, 1)
    per_inst = lambda p: jnp.tile(p.reshape(nh_r, HEAD_DIM).T, (1, B))
    scan_in = [to_scan(t) for t in (r_t, k_t, v_t, w_t, a_t)]
    y_scan = _wkv_scan(*scan_in, per_inst(rwkv_k_k), per_inst(rwkv_k_a), per_inst(rwkv_r_k),
                       per_inst(rwkv_lnx_g), per_inst(rwkv_lnx_b), _pick(S, 32))
    y_rwkv = y_scan.transpose(0, 2, 1).reshape(S, B * c)
    y_fox = _fox_attention(qa.reshape(B, S, G), ka.reshape(B, S, G), vt, c_fox, _pick(S, 512)).reshape(T, c_fox)

    w_router = jnp.zeros((D, LANES), F32)
    w_router = w_router.at[:, 0:N_GROUPS].set(moe_w_grp).at[:, E_ROW0:E_ROW0 + N_EXPERTS].set(moe_w_rt)
    b_router = jnp.zeros((1, LANES), F32)
    b_router = b_router.at[0, 0:N_GROUPS].set(moe_b_grp).at[0, E_ROW0:E_ROW0 + N_EXPERTS].set(moe_b_rt)
    x1, h2_a, h2_b, logits = _mix(y_rwkv, g_t, y_fox, x2, mod3, w_out.astype(BF16), norm2_g.reshape(1, D),
                          _split_bf16(w_router), S, _pick(S, 256))

    ids, wtok, cnt = _route(logits, b_router, _pick(T, 1024))
    counts = cnt[:, 0].astype(I32)
    padded = (counts + EXPERT_ROWS - 1) // EXPERT_ROWS * EXPERT_ROWS
    pends = jnp.cumsum(padded).astype(I32)
    pstarts = pends - padded
    n_rows = (2 * T + N_EXPERTS * (EXPERT_ROWS - 1) + EXPERT_ROWS - 1) // EXPERT_ROWS * EXPERT_ROWS
    n_blocks = n_rows // EXPERT_ROWS
    blk_start = jnp.arange(n_blocks, dtype=I32) * EXPERT_ROWS
    blk_e = jnp.sum((pends[None, :] <= blk_start[:, None]).astype(I32), axis=1)
    blk_e = jnp.minimum(blk_e, N_EXPERTS - 1)
    n_used = pends[-1:] // EXPERT_ROWS

    blk_first = blk_start - pstarts[blk_e]
    n_valid = jnp.clip(counts[blk_e] - blk_first, 0, EXPERT_ROWS).astype(I32)

    dest = _dest_rows(ids, pstarts, _pick(T, 1024))
    idx = jnp.concatenate([dest[0], dest[1]])
    xs_a = _sc_row_scatter(h2_a, idx, n_rows)
    xs_b = _sc_row_scatter(h2_b, idx, n_rows)
    ys_parts = _experts(blk_e, n_used, n_valid, xs_a, xs_b, moe_w_gate, moe_w_up, moe_w_down, EXPERT_ROWS)
    gathered = [_sc_row_gather(part, idx) for part in ys_parts]
    return x1, gathered, wtok


def kernel(x, c, w_ada, b_ada, norm1_g, w_in, rwkv_mu, rwkv_w0, rwkv_w_up, rwkv_a0, rwkv_a_up, rwkv_g_up, rwkv_k_k, rwkv_k_a, rwkv_r_k, rwkv_lnx_g, rwkv_lnx_b, fox_f_bias, w_out, norm2_g, moe_w_grp, moe_b_grp, moe_w_rt, moe_b_rt, moe_w_gate, moe_w_up, moe_w_down, norm_f_g):
    B, S, D = x.shape
    assert w_ada.shape[0] == 1, "single-layer model"
    mod3 = _adaln_mod(c, w_ada[0], b_ada[0]).reshape(B, 6, D)
    x1, gathered, wtok = _layer(
        x, mod3, norm1_g[0], w_in[0], rwkv_mu[0], rwkv_w0[0], rwkv_w_up[0], rwkv_a0[0], rwkv_a_up[0],
        rwkv_g_up[0], rwkv_k_k[0], rwkv_k_a[0], rwkv_r_k[0], rwkv_lnx_g[0], rwkv_lnx_b[0], fox_f_bias[0],
        w_out[0], norm2_g[0], moe_w_grp[0], moe_b_grp[0], moe_w_rt[0], moe_b_rt[0], moe_w_gate[0],
        moe_w_up[0], moe_w_down[0])
    out = _combine(gathered, wtok, x1, mod3, norm_f_g.reshape(1, D), S, _pick(B * S, 256))
    return out.reshape(B, S, D)
```

```python
import functools

import jax
import jax.numpy as jnp
from jax import lax
from jax.experimental import pallas as pl
from jax.experimental.pallas import tpu as pltpu
from jax.experimental.pallas import tpu_sc as plsc

F32 = jnp.float32
BF16 = jnp.bfloat16
I32 = jnp.int32
HIGHEST = lax.Precision.HIGHEST

HEAD_DIM = 64
N_GROUPS = 4
EXPERTS_PER_GROUP = 8
N_EXPERTS = N_GROUPS * EXPERTS_PER_GROUP
NORM_EPS = 1e-6
GN_EPS = 64e-5
LANES = 128
VMEM_LIMIT = 56 * 1024 * 1024

EXPERT_ROWS = 512
MASK_VALUE = -1e30


def _cparams(sem):
    return pltpu.CompilerParams(dimension_semantics=sem, vmem_limit_bytes=VMEM_LIMIT)


def _sigmoid(x):
    return 1.0 / (1.0 + jnp.exp(-x))


def _softplus(x):
    return jnp.maximum(x, 0.0) + jnp.log(1.0 + jnp.exp(-jnp.abs(x)))


def _split_bf16(w):
    hi = w.astype(BF16)
    return jnp.stack([hi, (w - hi.astype(F32)).astype(BF16)])


def _dot_split(a, w_ref):
    a_hi = a.astype(BF16)
    a_lo = (a - a_hi.astype(F32)).astype(BF16)
    w_hi = w_ref[0]
    return (jnp.dot(a_hi, w_hi, preferred_element_type=F32) + jnp.dot(a_lo, w_hi, preferred_element_type=F32)
            + jnp.dot(a_hi, w_ref[1], preferred_element_type=F32))


def _rms(x):
    return x * lax.rsqrt(jnp.mean(x * x, axis=-1, keepdims=True) + NORM_EPS)


def _mod_kernel(c_ref, w_ref, b_ref, o_ref):
    c = c_ref[...]
    cond = c * _sigmoid(c)
    o_ref[...] = jnp.dot(cond, w_ref[...], precision=HIGHEST, preferred_element_type=F32) + b_ref[...]


def _adaln_mod(c, w_ada, b_ada):
    B, D = c.shape
    N = w_ada.shape[1]
    tn = D
    return pl.pallas_call(
        _mod_kernel,
        grid=(N // tn,),
        in_specs=[pl.BlockSpec((B, D), lambda j: (0, 0)),
                  pl.BlockSpec((D, tn), lambda j: (0, j)),
                  pl.BlockSpec((1, tn), lambda j: (0, j))],
        out_specs=pl.BlockSpec((B, tn), lambda j: (0, j)),
        out_shape=jax.ShapeDtypeStruct((B, N), F32),
        compiler_params=_cparams(("arbitrary",)),
        name="adaln_mod",
    )(c, w_ada, b_ada.reshape(1, N))


def _proj_kernel(x_ref, mod_ref, g_ref, w_ref, pr_ref, q_ref, k_ref, v_ref, f_ref, *, n_rwkv, c_fox):
    m = mod_ref[0]
    h = (_rms(x_ref[...]) * g_ref[...] * (1.0 + m[1:2]) + m[0:1]).astype(BF16)
    o = n_rwkv
    pr_ref[...] = jnp.dot(h, w_ref[:, 0:o], preferred_element_type=F32)
    q = jnp.dot(h, w_ref[:, o:o + c_fox], preferred_element_type=F32)
    q_ref[...] = (q * (LOG2E * HEAD_DIM ** -0.5)).astype(BF16)
    k_ref[...] = jnp.dot(h, w_ref[:, o + c_fox:o + 2 * c_fox], preferred_element_type=F32).astype(BF16)
    v_ref[...] = jnp.dot(h, w_ref[:, o + 2 * c_fox:o + 3 * c_fox], preferred_element_type=F32).astype(BF16)
    f_ref[...] = jnp.dot(h, w_ref[:, o + 3 * c_fox:o + 3 * c_fox + LANES], preferred_element_type=F32)


def _rwkv_prep_kernel(p_ref, mu_ref, w0_ref, wup_ref, a0_ref, aup_ref, gup_ref,
                      r_ref, k_ref, v_ref, w_ref, a_ref, g_ref, carry, *, c):
    p = p_ref[...]
    tt = p.shape[0]
    row = lax.broadcasted_iota(I32, (tt, 1), 0)
    prev = jnp.where(row == 0, carry[...], pltpu.roll(p, 1, 0))
    carry[...] = p[tt - 1:tt, :]
    xs = p + (prev - p) * mu_ref[...]
    r_ref[...] = xs[:, 0:c]
    k_ref[...] = xs[:, c:2 * c]
    v_ref[...] = xs[:, 2 * c:3 * c]
    lo = xs[:, 3 * c:3 * c + LANES]
    wl = _dot_split(jnp.tanh(lo), wup_ref)
    w_log = -_softplus(-(w0_ref[...] + wl)) - 0.5
    w_ref[...] = jnp.exp(-jnp.exp(w_log))
    al = _dot_split(lo, aup_ref)
    a_ref[...] = _sigmoid(a0_ref[...] + al)
    gd = _sigmoid(xs[:, 3 * c + LANES:3 * c + 2 * LANES])
    g_ref[...] = _dot_split(gd, gup_ref)


ROW_UNROLL = 8


def _scan_kernel(r_ref, k_ref, v_ref, w_ref, a_ref, kkc_ref, kac_ref, rkc_ref, lg_ref, lb_ref,
                 y_ref, st, kk_s, b_s, km_s, rr_s, wprev_s, winc_s, yraw):
    @pl.when(pl.program_id(0) == 0)
    def _():
        st[...] = jnp.zeros_like(st)

    tc, n, lanes = r_ref.shape
    k = k_ref[...]
    a = a_ref[...]
    r = r_ref[...]
    kkr = k * kkc_ref[...][None]
    nrm = jnp.sqrt(jnp.sum(kkr * kkr, axis=1, keepdims=True))
    kk = kkr / jnp.maximum(nrm, 1e-12)
    km = k * (1.0 + (a - 1.0) * kac_ref[...][None])
    y_ref[...] = jnp.sum(r * km * rkc_ref[...][None], axis=1, keepdims=True) * v_ref[...]

    def cumulate(t, wc):
        wprev_s[t] = wc
        wc = wc * w_ref[t]
        winc_s[t] = wc
        return wc

    w_chunk = lax.fori_loop(0, tc, cumulate, jnp.ones((n, lanes), F32))
    winc = winc_s[...]
    inv = 1.0 / winc
    kk_s[...] = -(wprev_s[...] * kk)
    b_s[...] = kk * a * inv
    km_s[...] = km * inv
    rr_s[...] = r * winc

    def step(t, carry):
        kk_t = kk_s[t]
        b_t = b_s[t]
        k_t = km_s[t]
        r_t = rr_s[t]

        def rows(j, c2):
            for u in range(ROW_UNROLL):
                vv = j * ROW_UNROLL + u
                s_old = st[vv]
                sa = jnp.sum(s_old * kk_t, axis=0, keepdims=True)
                s_new = s_old + sa * b_t + v_ref[t, pl.ds(vv, 1), :] * k_t
                st[vv] = s_new
                yraw[t, pl.ds(vv, 1), :] = jnp.sum(s_new * r_t, axis=0, keepdims=True)
            return c2

        lax.fori_loop(0, n // ROW_UNROLL, rows, 0, unroll=True)
        return carry

    lax.fori_loop(0, tc, step, 0)
    st[...] = st[...] * w_chunk[None]

    y = yraw[...]
    mean = jnp.mean(y, axis=1, keepdims=True)
    yc = y - mean
    var = jnp.mean(yc * yc, axis=1, keepdims=True)
    yn = yc * lax.rsqrt(var + GN_EPS) * lg_ref[...][None] + lb_ref[...][None]
    y_ref[...] = y_ref[...] + yn


def _wkv_scan(r, k, v, w, a, kkc, kac, rkc, lg, lb, tc):
    S, n, L = r.shape
    blk = pl.BlockSpec((tc, n, L), lambda i: (i, 0, 0))
    cst = pl.BlockSpec((n, L), lambda i: (0, 0))
    return pl.pallas_call(
        _scan_kernel,
        grid=(S // tc,),
        in_specs=[blk] * 5 + [cst] * 5,
        out_specs=blk,
        out_shape=jax.ShapeDtypeStruct((S, n, L), F32),
        scratch_shapes=[pltpu.VMEM((n, n, L), F32)] + [pltpu.VMEM((tc, n, L), F32)] * 7,
        compiler_params=_cparams(("arbitrary",)),
        name="wkv_scan",
    )(r, k, v, w, a, kkc, kac, rkc, lg, lb)


LOG2E = 1.4426950408889634
N_PIECES = 3


def _aux_base(h):
    return (1 - h % 2) * HEAD_DIM


def _fox_prep_kernel(q_ref, k_ref, v_ref, f_ref, fb_ref, tri_ref, sel_ref, mq_ref, cq_ref, ck_ref,
                     qa_ref, ka_ref, vt_ref, carry, *, nh):
    lane1 = lax.broadcasted_iota(I32, (1, LANES), 1)

    def pack_pieces(x):
        pieces, rest = [], x
        for _ in range(N_PIECES):
            piece = rest.astype(BF16).astype(F32)
            pieces.append(piece)
            rest = rest - piece
        packed = jnp.where(lane1 < nh, pieces[0],
                           jnp.where(lane1 < 2 * nh, pltpu.roll(pieces[1], nh, 1),
                                     jnp.where(lane1 < 3 * nh, pltpu.roll(pieces[2], 2 * nh, 1), 0.0)))
        return packed.astype(BF16)

    z = f_ref[...] + fb_ref[...]
    lf = -_softplus(-z)
    part = jnp.dot(tri_ref[...], pack_pieces(lf), preferred_element_type=F32)
    inc = part + pltpu.roll(part, LANES - nh, 1) + pltpu.roll(part, LANES - 2 * nh, 1) + carry[...]
    ts = inc.shape[0]
    carry[...] = inc[ts - 1:ts, :]

    aux = jnp.dot(pack_pieces(inc * LOG2E), sel_ref[...], preferred_element_type=F32)
    aux_q = jnp.where(mq_ref[...] > 0, aux, cq_ref[...])
    aux_k = jnp.where(mq_ref[...] < 0, aux, ck_ref[...])

    lane = lax.broadcasted_iota(I32, (1, nh * LANES), 1)
    own = ((lane // HEAD_DIM) % 2) == ((lane // LANES) % 2)
    q = q_ref[...]
    k = k_ref[...]
    dup = lambda t: jnp.concatenate(
        [t[:, (h // 2) * LANES:(h // 2 + 1) * LANES] for h in range(nh)], axis=1)
    qa_ref[...] = jnp.where(own, dup(q), aux_q.astype(BF16))
    ka_ref[...] = jnp.where(own, dup(k), aux_k.astype(BF16))

    vt = v_ref[...].astype(F32).T
    row = lax.broadcasted_iota(I32, (LANES, 1), 0)
    groups = []
    for h in range(nh):
        pair_rows = vt[(h // 2) * LANES:(h // 2 + 1) * LANES]
        own_rows = (row // HEAD_DIM) == (h % 2)
        ones_row = (row == _aux_base(h)).astype(F32)
        groups.append(jnp.where(own_rows, pair_rows, ones_row))
    vt_ref[0] = jnp.concatenate(groups, axis=0).astype(BF16)


def _fox_prep_constants(nh, ts):
    G = nh * LANES
    tri = (lax.broadcasted_iota(I32, (ts, ts), 0) >= lax.broadcasted_iota(I32, (ts, ts), 1)).astype(BF16)
    sel = jnp.zeros((LANES, G), F32)
    mq = jnp.zeros((1, G), F32)
    cq = jnp.zeros((1, G), F32)
    ck = jnp.zeros((1, G), F32)
    for h in range(nh):
        base = h * LANES + _aux_base(h)
        for pc in range(N_PIECES):
            sel = sel.at[pc * nh + h, base + pc].set(1.0).at[pc * nh + h, base + N_PIECES + pc].set(-1.0)
            mq = mq.at[0, base + pc].set(1.0).at[0, base + N_PIECES + pc].set(-1.0)
            ck = ck.at[0, base + pc].set(1.0)
            cq = cq.at[0, base + N_PIECES + pc].set(1.0)
    return tri, sel.astype(BF16), mq, cq, ck


def _front_kernel(x_ref, mod_ref, ng_ref, w_ref, mu_ref, w0_ref, wup_ref, a0_ref, aup_ref, gup_ref,
                  fb_ref, tri_ref, sel_ref, mq_ref, cq_ref, ck_ref,
                  r_ref, k_ref, v_ref, wd_ref, a_ref, g_ref, qa_ref, ka_ref, vt_ref,
                  pr_s, q_s, kf_s, vf_s, f_s, carry_p, carry_f, *, n_rwkv, c_fox, c, nh, tm):
    @pl.when(pl.program_id(1) == 0)
    def _():
        carry_p[...] = jnp.zeros_like(carry_p)
        carry_f[...] = jnp.zeros_like(carry_f)

    for u in range(x_ref.shape[0] // tm):
        rows = pl.ds(u * tm, tm)
        s = u % 2
        _proj_kernel(x_ref.at[rows], mod_ref, ng_ref, w_ref, pr_s.at[s], q_s.at[s], kf_s.at[s], vf_s.at[s],
                     f_s.at[s], n_rwkv=n_rwkv, c_fox=c_fox)
        _rwkv_prep_kernel(pr_s.at[s], mu_ref, w0_ref, wup_ref, a0_ref, aup_ref, gup_ref,
                          r_ref.at[rows], k_ref.at[rows], v_ref.at[rows], wd_ref.at[rows], a_ref.at[rows],
                          g_ref.at[rows], carry_p, c=c)
        _fox_prep_kernel(q_s.at[s], kf_s.at[s], vf_s.at[s], f_s.at[s], fb_ref, tri_ref, sel_ref, mq_ref,
                         cq_ref, ck_ref, qa_ref.at[rows], ka_ref.at[rows], vt_ref.at[:, :, rows], carry_f, nh=nh)


FRONT_SUBTILES = 2


def _front(x2, mod3, norm_g, w_pad, mu, w0, wup, a0, aup, gup, fb_pad, B, S, n_rwkv, c, c_fox, nh, tm):
    T, D = x2.shape
    NW = w_pad.shape[1]
    G = nh * LANES
    tb = tm * FRONT_SUBTILES if S % (tm * FRONT_SUBTILES) == 0 else tm
    nt = S // tb
    tri, sel, mq, cq, ck = _fox_prep_constants(nh, tm)
    row = lambda b, i: (b * nt + i, 0)
    const = lambda b, i: (0, 0)
    const3 = lambda b, i: (0, 0, 0)
    tmaj = pl.BlockSpec((tb, c), lambda b, i: (i, b))
    return pl.pallas_call(
        functools.partial(_front_kernel, n_rwkv=n_rwkv, c_fox=c_fox, c=c, nh=nh, tm=tm),
        grid=(B, nt),
        in_specs=[pl.BlockSpec((tb, D), row), pl.BlockSpec((1, 6, D), lambda b, i: (b, 0, 0)),
                  pl.BlockSpec((1, D), const), pl.BlockSpec((D, NW), const),
                  pl.BlockSpec((1, n_rwkv), const), pl.BlockSpec((1, c), const),
                  pl.BlockSpec((2, LANES, c), const3), pl.BlockSpec((1, c), const),
                  pl.BlockSpec((2, LANES, c), const3), pl.BlockSpec((2, LANES, c), const3),
                  pl.BlockSpec((1, LANES), const), pl.BlockSpec((tm, tm), const),
                  pl.BlockSpec((LANES, G), const), pl.BlockSpec((1, G), const),
                  pl.BlockSpec((1, G), const), pl.BlockSpec((1, G), const)],
        out_specs=[tmaj] * 5 + [pl.BlockSpec((tb, c), row), pl.BlockSpec((tb, G), row),
                                pl.BlockSpec((tb, G), row), pl.BlockSpec((1, G, tb), lambda b, i: (b, 0, i))],
        out_shape=[jax.ShapeDtypeStruct((S, B * c), F32)] * 5
        + [jax.ShapeDtypeStruct((T, c), F32), jax.ShapeDtypeStruct((T, G), BF16),
           jax.ShapeDtypeStruct((T, G), BF16), jax.ShapeDtypeStruct((B, G, S), BF16)],
        scratch_shapes=[pltpu.VMEM((2, tm, n_rwkv), F32), pltpu.VMEM((2, tm, c_fox), BF16),
                        pltpu.VMEM((2, tm, c_fox), BF16), pltpu.VMEM((2, tm, c_fox), BF16),
                        pltpu.VMEM((2, tm, LANES), F32), pltpu.VMEM((1, n_rwkv), F32), pltpu.VMEM((1, LANES), F32)],
        compiler_params=_cparams(("arbitrary", "arbitrary")),
        name="front_proj_prep",
    )(x2, mod3, norm_g, w_pad, mu, w0, wup, a0, aup, gup, fb_pad, tri, sel, mq, cq, ck)


def _fox_kernel(q_ref, k_ref, vt_ref, o_ref, acc, mrow, s_a, s_b, lim, *, tq):
    qi = pl.program_id(2)
    tk = tq // 2
    n = 2 * (qi + 1)
    acc[...] = jnp.zeros_like(acc)
    mrow[...] = jnp.full_like(mrow, MASK_VALUE)

    @pl.when((pl.program_id(0) == 0) & (pl.program_id(1) == 0) & (qi == 0))
    def _():
        key = lax.broadcasted_iota(I32, (tk, tq), 0)
        query = lax.broadcasted_iota(I32, (tk, tq), 1)
        lim[0] = jnp.where(key <= query, -MASK_VALUE, MASK_VALUE)
        lim[1] = jnp.where(key + tk <= query, -MASK_VALUE, MASK_VALUE)

    def scores(kb, dst):
        off = pl.multiple_of(jnp.minimum(kb, n - 1) * tk, tk)
        for hh in range(2):
            grp = slice(hh * LANES, (hh + 1) * LANES)
            dst[hh] = lax.dot_general(k_ref[0, pl.ds(off, tk), grp], q_ref[0, :, grp],
                                      (((1,), (1,)), ((), ())), preferred_element_type=F32)

    def softmax_pv(kb, src, diag=None):
        off = pl.multiple_of(kb * tk, tk)
        for hh in range(2):
            grp = slice(hh * LANES, (hh + 1) * LANES)
            st = src[hh] if diag is None else jnp.minimum(src[hh], lim[diag])
            m_old = mrow[hh]
            m_new = jnp.maximum(m_old, jnp.max(st, axis=0, keepdims=True))
            mrow[hh] = m_new
            p = jnp.exp2(st - m_new).astype(BF16)
            pv = jnp.dot(vt_ref[0, grp, pl.ds(off, tk)], p, preferred_element_type=F32)
            acc[hh] = jnp.exp2(m_old - m_new) * acc[hh] + pv

    scores(0, s_a)

    def trip(j, c):
        scores(2 * j + 1, s_b)
        softmax_pv(2 * j, s_a)
        scores(2 * j + 2, s_a)
        softmax_pv(2 * j + 1, s_b)
        return c

    def double_trip(j, c):
        return trip(2 * j + 1, trip(2 * j, c))

    lax.fori_loop(0, qi // 2, double_trip, 0)

    @pl.when(qi % 2 == 1)
    def _():
        trip(qi - 1, 0)
    scores(n - 1, s_b)
    softmax_pv(n - 2, s_a, diag=0)
    softmax_pv(n - 1, s_b, diag=1)

    row = lax.broadcasted_iota(I32, (LANES, 1), 0)
    out_t = jnp.zeros((LANES, tq), F32)
    for hh in range(2):
        a = acc[hh]
        base = _aux_base(hh)
        l = a[base:base + 1, :]
        out_t = jnp.where((row // HEAD_DIM) == hh, a / l, out_t)
    o_ref[0] = out_t.T


def _fox_attention(qa, ka, vt, c_fox, tq):
    B, S, G = qa.shape
    nh = G // LANES
    return pl.pallas_call(
        functools.partial(_fox_kernel, tq=tq),
        grid=(B, nh // 2, S // tq),
        in_specs=[pl.BlockSpec((1, tq, 2 * LANES), lambda b, p, i: (b, i, p)),
                  pl.BlockSpec((1, S, 2 * LANES), lambda b, p, i: (b, 0, p)),
                  pl.BlockSpec((1, 2 * LANES, S), lambda b, p, i: (b, p, 0))],
        out_specs=pl.BlockSpec((1, tq, LANES), lambda b, p, i: (b, i, p)),
        out_shape=jax.ShapeDtypeStruct((B, S, c_fox), F32),
        scratch_shapes=[pltpu.VMEM((2, LANES, tq), F32), pltpu.VMEM((2, 1, tq), F32),
                        pltpu.VMEM((2, tq // 2, tq), F32), pltpu.VMEM((2, tq // 2, tq), F32),
                        pltpu.VMEM((2, tq // 2, tq), F32)],
        compiler_params=_cparams(("arbitrary", "arbitrary", "arbitrary")),
        name="fox_attention",
    )(qa, ka, vt)


def _mix_kernel(yr_ref, g_ref, yf_ref, x_ref, mod_ref, wo_ref, n2_ref, wr_ref,
                x1_ref, h2a_ref, h2b_ref, lg_ref, *, c):
    m = mod_ref[0]
    a = (yr_ref[...] * g_ref[...]).astype(BF16)
    b = yf_ref[...].astype(BF16)
    mix = (jnp.dot(a, wo_ref[0:c, :], preferred_element_type=F32)
           + jnp.dot(b, wo_ref[c:, :], preferred_element_type=F32))
    x1 = x_ref[...] + m[2:3] * mix
    x1_ref[...] = x1
    h2 = _rms(x1) * n2_ref[...] * (1.0 + m[4:5]) + m[3:4]
    words = _pack_bf16_halves(h2)
    hw = words.shape[1] // 2
    h2a_ref[...] = words[:, :hw]
    h2b_ref[...] = words[:, hw:]
    lg_ref[...] = _dot_split(h2, wr_ref)


def _pack_bf16_halves(x):
    w = x.shape[1] // 2
    bits = lax.bitcast_convert_type(x.astype(BF16).astype(F32), jnp.uint32)
    return (bits[:, :w] >> 16) | (bits[:, w:] & jnp.uint32(0xFFFF0000))


def _unpack_bf16_halves(words):
    lo = lax.bitcast_convert_type(words << 16, F32)
    hi = lax.bitcast_convert_type(words & jnp.uint32(0xFFFF0000), F32)
    return jnp.concatenate([lo, hi], axis=1).astype(BF16)


def _mix(y_rwkv, g, y_fox, x2, mod3, w_out, norm2_g, w_router, S, tm):
    T, D = x2.shape
    c = g.shape[1]
    nt = S // tm
    row = lambda i: (i, 0)
    const = lambda i: (0, 0)
    return pl.pallas_call(
        functools.partial(_mix_kernel, c=c),
        grid=(T // tm,),
        in_specs=[pl.BlockSpec((tm, c), lambda i: (i % nt, i // nt)),
                  pl.BlockSpec((tm, c), row), pl.BlockSpec((tm, c), row),
                  pl.BlockSpec((tm, D), row),
                  pl.BlockSpec((1, 6, D), lambda i: ((i * tm) // S, 0, 0)),
                  pl.BlockSpec((D, D), const), pl.BlockSpec((1, D), const),
                  pl.BlockSpec((2, D, LANES), lambda i: (0, 0, 0))],
        out_specs=[pl.BlockSpec((tm, D), row), pl.BlockSpec((tm, D // 4), row), pl.BlockSpec((tm, D // 4), row),
                   pl.BlockSpec((tm, LANES), row)],
        out_shape=[jax.ShapeDtypeStruct((T, D), F32), jax.ShapeDtypeStruct((T, D // 4), jnp.uint32),
                   jax.ShapeDtypeStruct((T, D // 4), jnp.uint32),
                   jax.ShapeDtypeStruct((T, LANES), F32)],
        compiler_params=_cparams(("arbitrary",)),
        name="out_proj_norm2_router",
    )(y_rwkv, g, y_fox, x2, mod3, w_out, norm2_g, w_router)


E_ROW0 = 8


def _first_argmax(vals, n):
    mx = jnp.max(vals, axis=0, keepdims=True)
    idx = lax.broadcasted_iota(I32, vals.shape, 0).astype(F32)
    first = jnp.min(jnp.where(vals == mx, idx, float(n)), axis=0, keepdims=True)
    return first.astype(I32), mx


def _route_kernel(lg_ref, bias_ref, tri_ref, ids_ref, wtok_ref, cnt_ref, carry):
    @pl.when(pl.program_id(0) == 0)
    def _():
        carry[...] = jnp.zeros_like(carry)

    lt = (lg_ref[...] + bias_ref[...]).T
    tm = lt.shape[1]
    grp = lt[0:N_GROUPS]
    ge = jnp.exp(grp - jnp.max(grp, axis=0, keepdims=True))
    gp = ge / jnp.sum(ge, axis=0, keepdims=True)
    g_sel, p_g = _first_argmax(gp, N_GROUPS)
    sel = jnp.zeros((EXPERTS_PER_GROUP, tm), F32)
    for g in range(N_GROUPS):
        lo = E_ROW0 + g * EXPERTS_PER_GROUP
        sel = jnp.where(g_sel == g, lt[lo:lo + EXPERTS_PER_GROUP], sel)
    ee = jnp.exp(sel - jnp.max(sel, axis=0, keepdims=True))
    ep = ee / jnp.sum(ee, axis=0, keepdims=True)
    i0, p0 = _first_argmax(ep, EXPERTS_PER_GROUP)
    idx8 = lax.broadcasted_iota(I32, ep.shape, 0)
    i1, p1 = _first_argmax(jnp.where(idx8 == i0, -1.0, ep), EXPERTS_PER_GROUP)
    den = p0 + p1
    w0 = p_g * p0 / den
    w1 = p_g * p1 / den
    e0 = g_sel * EXPERTS_PER_GROUP + i0
    e1 = g_sel * EXPERTS_PER_GROUP + i1

    ide = lax.broadcasted_iota(I32, (N_EXPERTS, tm), 0)
    oh0 = ide == e0
    oh1 = ide == e1
    oh = oh0.astype(F32) + oh1.astype(F32)
    incl = jnp.dot(oh.astype(BF16), tri_ref[...], preferred_element_type=F32)
    base = carry[...] + (incl - oh)
    r0 = jnp.sum(jnp.where(oh0, base, 0.0), axis=0, keepdims=True)
    r1 = jnp.sum(jnp.where(oh1, base, 0.0), axis=0, keepdims=True)
    carry[...] = carry[...] + incl[:, tm - 1:tm]
    cnt_ref[...] = jnp.broadcast_to(carry[...], cnt_ref.shape)
    ids_ref[...] = jnp.concatenate(
        [e0, e1, r0.astype(I32), r1.astype(I32), jnp.zeros((4, tm), I32)], axis=0)
    wtok_ref[...] = jnp.concatenate([w0, w1, jnp.zeros((LANES - 2, tm), F32)], axis=0).T


def _route(logits, bias_row, tm):
    T = logits.shape[0]
    tri = (lax.broadcasted_iota(I32, (tm, tm), 0) <= lax.broadcasted_iota(I32, (tm, tm), 1)).astype(BF16)
    return pl.pallas_call(
        _route_kernel,
        grid=(T // tm,),
        in_specs=[pl.BlockSpec((tm, LANES), lambda i: (i, 0)),
                  pl.BlockSpec((1, LANES), lambda i: (0, 0)),
                  pl.BlockSpec((tm, tm), lambda i: (0, 0))],
        out_specs=[pl.BlockSpec((8, tm), lambda i: (0, i)),
                   pl.BlockSpec((tm, LANES), lambda i: (i, 0)),
                   pl.BlockSpec((N_EXPERTS, LANES), lambda i: (0, 0))],
        out_shape=[jax.ShapeDtypeStruct((8, T), I32), jax.ShapeDtypeStruct((T, LANES), F32),
                   jax.ShapeDtypeStruct((N_EXPERTS, LANES), F32)],
        scratch_shapes=[pltpu.VMEM((N_EXPERTS, 1), F32)],
        compiler_params=_cparams(("arbitrary",)),
        name="route_rank",
    )(logits, bias_row, tri)


def _dest_kernel(ids_ref, ps_ref, d_ref):
    ids = ids_ref[...]
    tm = ids.shape[1]
    ide = lax.broadcasted_iota(I32, (N_EXPERTS, tm), 0)
    ps = ps_ref[...]
    rows = [jnp.sum(jnp.where(ide == ids[k:k + 1], ps, 0), axis=0, keepdims=True) + ids[2 + k:3 + k]
            for k in range(2)]
    d_ref[...] = jnp.concatenate(rows + [jnp.zeros((6, tm), I32)], axis=0)


def _dest_rows(ids, pstarts, tm):
    T = ids.shape[1]
    return pl.pallas_call(
        _dest_kernel,
        grid=(T // tm,),
        in_specs=[pl.BlockSpec((8, tm), lambda i: (0, i)), pl.BlockSpec((N_EXPERTS, 1), lambda i: (0, 0))],
        out_specs=pl.BlockSpec((8, tm), lambda i: (0, i)),
        out_shape=jax.ShapeDtypeStruct((8, T), I32),
        compiler_params=_cparams(("arbitrary",)),
        name="dest_rows",
    )(ids, pstarts.reshape(N_EXPERTS, 1))


Y_PARTS = 4


def _expert_kernel(be_ref, nu_ref, nv_ref, xa_ref, xb_ref, wg_ref, wu_ref, wd_ref, *rest):
    y_refs, (wg_s, wu_s, wd_s) = rest[:Y_PARTS], rest[Y_PARTS:]
    dq = y_refs[0].shape[1]
    i = pl.program_id(0)
    used = i < nu_ref[0]

    @pl.when(used & ((i == 0) | (be_ref[i] != be_ref[jnp.maximum(i - 1, 0)])))
    def _():
        wg_s[...] = wg_ref[0].astype(BF16)
        wu_s[...] = wu_ref[0].astype(BF16)
        wd_s[...] = wd_ref[0].astype(BF16)

    @pl.when(used)
    def _():
        x = _unpack_bf16_halves(jnp.concatenate([xa_ref[...], xb_ref[...]], axis=1))
        row = lax.broadcasted_iota(I32, (x.shape[0], 1), 0)
        x = jnp.where(row < nv_ref[i], x, jnp.zeros_like(x))
        g = jnp.dot(x, wg_s[...], preferred_element_type=F32)
        u = jnp.dot(x, wu_s[...], preferred_element_type=F32)
        hid = (g * _sigmoid(g) * u).astype(BF16)
        for q, y_ref in enumerate(y_refs):
            y_ref[...] = jnp.dot(hid, wd_s[:, q * dq:(q + 1) * dq], preferred_element_type=F32)

    @pl.when(jnp.logical_not(used))
    def _():
        for y_ref in y_refs:
            y_ref[...] = jnp.zeros_like(y_ref)


def _experts(blk_e, n_used, n_valid, xs_a, xs_b, wg, wu, wd, tme):
    P, W = xs_a.shape
    _, D, F = wg.shape
    grid_spec = pltpu.PrefetchScalarGridSpec(
        num_scalar_prefetch=3,
        grid=(P // tme,),
        in_specs=[pl.BlockSpec((tme, W), lambda i, *_: (i, 0)),
                  pl.BlockSpec((tme, W), lambda i, *_: (i, 0)),
                  pl.BlockSpec((1, D, F), lambda i, be, *_: (be[i], 0, 0)),
                  pl.BlockSpec((1, D, F), lambda i, be, *_: (be[i], 0, 0)),
                  pl.BlockSpec((1, F, D), lambda i, be, *_: (be[i], 0, 0))],
        out_specs=[pl.BlockSpec((tme, D // Y_PARTS), lambda i, *_: (i, 0))] * Y_PARTS,
        scratch_shapes=[pltpu.VMEM((D, F), BF16), pltpu.VMEM((D, F), BF16), pltpu.VMEM((F, D), BF16)],
    )
    return pl.pallas_call(
        _expert_kernel,
        grid_spec=grid_spec,
        out_shape=[jax.ShapeDtypeStruct((P, D // Y_PARTS), F32)] * Y_PARTS,
        compiler_params=_cparams(("arbitrary",)),
        name="moe_experts",
    )(blk_e, n_used, n_valid, xs_a, xs_b, wg, wu, wd)


SC_GATHER_WINDOW = 128


def _sc_row_gather(table, idx):
    M = idx.shape[0]
    W = table.shape[1]
    mesh = plsc.VectorSubcoreMesh(core_axis_name="c", subcore_axis_name="s")

    @functools.partial(pl.kernel, out_type=jax.ShapeDtypeStruct((M, W), table.dtype), mesh=mesh)
    def gather_kernel(x_hbm, i_hbm, o_hbm):
        def body(i_vmem, o_vmem):
            pltpu.sync_copy(x_hbm.at[i_vmem.at[0]], o_vmem)

        pltpu.emit_pipeline(
            body,
            grid=(M // SC_GATHER_WINDOW,),
            in_specs=[pl.BlockSpec((1, SC_GATHER_WINDOW), lambda i: (0, i))],
            out_specs=[pl.BlockSpec((SC_GATHER_WINDOW, W), lambda i: (i, 0))],
            core_axis_name=("c", "s"),
            dimension_semantics=(pltpu.PARALLEL,),
        )(i_hbm, o_hbm)

    return gather_kernel(table, idx.reshape(1, M))


def _sc_row_scatter(rows, idx, n_out):
    R, W = rows.shape
    M = idx.shape[0]
    nr = R // SC_GATHER_WINDOW
    mesh = plsc.VectorSubcoreMesh(core_axis_name="c", subcore_axis_name="s")

    @functools.partial(pl.kernel, out_type=jax.ShapeDtypeStruct((n_out, W), rows.dtype), mesh=mesh)
    def scatter_kernel(x_hbm, i_hbm, o_hbm):
        def body(x_vmem, i_vmem):
            pltpu.sync_copy(x_vmem, o_hbm.at[i_vmem.at[0]])

        pltpu.emit_pipeline(
            body,
            grid=(M // SC_GATHER_WINDOW,),
            in_specs=[pl.BlockSpec((SC_GATHER_WINDOW, W), lambda i: (i % nr, 0)),
                      pl.BlockSpec((1, SC_GATHER_WINDOW), lambda i: (0, i))],
            out_specs=[],
            core_axis_name=("c", "s"),
            dimension_semantics=(pltpu.PARALLEL,),
        )(x_hbm, i_hbm)

    return scatter_kernel(rows, idx.reshape(1, M))


def _combine_kernel(*refs):
    g_refs = refs[:2 * Y_PARTS]
    wtok_ref, x1_ref, mod_ref, gf_ref, o_ref = refs[2 * Y_PARTS:]
    m = mod_ref[0]
    w = wtok_ref[...]
    y0 = jnp.concatenate([r[...] for r in g_refs[:Y_PARTS]], axis=1)
    y1 = jnp.concatenate([r[...] for r in g_refs[Y_PARTS:]], axis=1)
    ff = w[:, 0:1] * y0 + w[:, 1:2] * y1
    x2 = x1_ref[...] + m[5:6] * ff
    o_ref[...] = _rms(x2) * gf_ref[...]


def _combine(gathered, wtok, x1, mod3, norm_f_g, S, tmc):
    T, D = x1.shape
    n = T // tmc
    dq = D // Y_PARTS
    g_specs = [pl.BlockSpec((tmc, dq), lambda i, k=k: (i + k * n, 0)) for k in range(2) for _ in range(Y_PARTS)]
    g_args = [gathered[q] for _ in range(2) for q in range(Y_PARTS)]
    return pl.pallas_call(
        _combine_kernel,
        grid=(n,),
        in_specs=g_specs + [pl.BlockSpec((tmc, LANES), lambda i: (i, 0)),
                            pl.BlockSpec((tmc, D), lambda i: (i, 0)),
                            pl.BlockSpec((1, 6, D), lambda i: ((i * tmc) // S, 0, 0)),
                            pl.BlockSpec((1, D), lambda i: (0, 0))],
        out_specs=pl.BlockSpec((tmc, D), lambda i: (i, 0)),
        out_shape=jax.ShapeDtypeStruct((T, D), F32),
        compiler_params=_cparams(("arbitrary",)),
        name="moe_combine_final_norm",
    )(*g_args, wtok, x1, mod3, norm_f_g)


def _pick(n, pref):
    t = min(pref, n)
    while n % t:
        t //= 2
    return t


def _layer(x, mod3, norm1_g, w_in, rwkv_mu, rwkv_w0, rwkv_w_up, rwkv_a0, rwkv_a_up, rwkv_g_up, rwkv_k_k,
           rwkv_k_a, rwkv_r_k, rwkv_lnx_g, rwkv_lnx_b, fox_f_bias, w_out, norm2_g, moe_w_grp, moe_b_grp,
           moe_w_rt, moe_b_rt, moe_w_gate, moe_w_up, moe_w_down):
    B, S, D = x.shape
    T = B * S
    c = rwkv_w0.shape[0]
    nh_r = c // HEAD_DIM
    n_rwkv = rwkv_mu.shape[0]
    nh_f = fox_f_bias.shape[0]
    c_fox = nh_f * HEAD_DIM
    d_lora = rwkv_w_up.shape[0]
    x2 = x.reshape(T, D)

    w_pad = jnp.pad(w_in, ((0, 0), (0, LANES - nh_f))).astype(BF16)
    wup_pad = jnp.pad(rwkv_w_up, ((0, LANES - d_lora), (0, 0)))
    aup_pad = jnp.pad(rwkv_a_up, ((d_lora, LANES - d_lora - rwkv_a_up.shape[0]), (0, 0)))
    fb_pad = jnp.pad(fox_f_bias, (0, LANES - nh_f)).reshape(1, LANES)
    r_t, k_t, v_t, w_t, a_t, g_t, qa, ka, vt = _front(
        x2, mod3, norm1_g.reshape(1, D), w_pad, rwkv_mu.reshape(1, n_rwkv), rwkv_w0.reshape(1, c),
        _split_bf16(wup_pad), rwkv_a0.reshape(1, c), _split_bf16(aup_pad), _split_bf16(rwkv_g_up), fb_pad,
        B, S, n_rwkv, c, c_fox, nh_f, _pick(S, 256))
    G = nh_f * LANES

    inst = B * nh_r
    to_scan = lambda t: t.reshape(S, inst, HEAD_DIM).transpose(0, 2, 1)
    per_inst = lambda p: jnp.tile(p.reshape(nh_r, HEAD_DIM).T, (1, B))
    scan_in = [to_scan(t) for t in (r_t, k_t, v_t, w_t, a_t)]
    y_scan = _wkv_scan(*scan_in, per_inst(rwkv_k_k), per_inst(rwkv_k_a), per_inst(rwkv_r_k),
                       per_inst(rwkv_lnx_g), per_inst(rwkv_lnx_b), _pick(S, 32))
    y_rwkv = y_scan.transpose(0, 2, 1).reshape(S, B * c)
    y_fox = _fox_attention(qa.reshape(B, S, G), ka.reshape(B, S, G), vt, c_fox, _pick(S, 512)).reshape(T, c_fox)

    w_router = jnp.zeros((D, LANES), F32)
    w_router = w_router.at[:, 0:N_GROUPS].set(moe_w_grp).at[:, E_ROW0:E_ROW0 + N_EXPERTS].set(moe_w_rt)
    b_router = jnp.zeros((1, LANES), F32)
    b_router = b_router.at[0, 0:N_GROUPS].set(moe_b_grp).at[0, E_ROW0:E_ROW0 + N_EXPERTS].set(moe_b_rt)
    x1, h2_a, h2_b, logits = _mix(y_rwkv, g_t, y_fox, x2, mod3, w_out.astype(BF16), norm2_g.reshape(1, D),
                          _split_bf16(w_router), S, _pick(S, 256))

    ids, wtok, cnt = _route(logits, b_router, _pick(T, 1024))
    counts = cnt[:, 0].astype(I32)
    padded = (counts + EXPERT_ROWS - 1) // EXPERT_ROWS * EXPERT_ROWS
    pends = jnp.cumsum(padded).astype(I32)
    pstarts = pends - padded
    n_rows = (2 * T + N_EXPERTS * (EXPERT_ROWS - 1) + EXPERT_ROWS - 1) // EXPERT_ROWS * EXPERT_ROWS
    n_blocks = n_rows // EXPERT_ROWS
    blk_start = jnp.arange(n_blocks, dtype=I32) * EXPERT_ROWS
    blk_e = jnp.sum((pends[None, :] <= blk_start[:, None]).astype(I32), axis=1)
    blk_e = jnp.minimum(blk_e, N_EXPERTS - 1)
    n_used = pends[-1:] // EXPERT_ROWS

    blk_first = blk_start - pstarts[blk_e]
    n_valid = jnp.clip(counts[blk_e] - blk_first, 0, EXPERT_ROWS).astype(I32)

    dest = _dest_rows(ids, pstarts, _pick(T, 1024))
    idx = jnp.concatenate([dest[0], dest[1]])
    xs_a = _sc_row_scatter(h2_a, idx, n_rows)
    xs_b = _sc_row_scatter(h2_b, idx, n_rows)
    ys_parts = _experts(blk_e, n_used, n_valid, xs_a, xs_b, moe_w_gate, moe_w_up, moe_w_down, EXPERT_ROWS)
    gathered = [_sc_row_gather(part, idx) for part in ys_parts]
    return x1, gathered, wtok


def kernel(x, c, w_ada, b_ada, norm1_g, w_in, rwkv_mu, rwkv_w0, rwkv_w_up, rwkv_a0, rwkv_a_up, rwkv_g_up, rwkv_k_k, rwkv_k_a, rwkv_r_k, rwkv_lnx_g, rwkv_lnx_b, fox_f_bias, w_out, norm2_g, moe_w_grp, moe_b_grp, moe_w_rt, moe_b_rt, moe_w_gate, moe_w_up, moe_w_down, norm_f_g):
    B, S, D = x.shape
    assert w_ada.shape[0] == 1, "single-layer model"
    mod3 = _adaln_mod(c, w_ada[0], b_ada[0]).reshape(B, 6, D)
    x1, gathered, wtok = _layer(
        x, mod3, norm1_g[0], w_in[0], rwkv_mu[0], rwkv_w0[0], rwkv_w_up[0], rwkv_a0[0], rwkv_a_up[0],
        rwkv_g_up[0], rwkv_k_k[0], rwkv_k_a[0], rwkv_r_k[0], rwkv_lnx_g[0], rwkv_lnx_b[0], fox_f_bias[0],
        w_out[0], norm2_g[0], moe_w_grp[0], moe_b_grp[0], moe_w_rt[0], moe_b_rt[0], moe_w_gate[0],
        moe_w_up[0], moe_w_down[0])
    out = _combine(gathered, wtok, x1, mod3, norm_f_g.reshape(1, D), S, _pick(B * S, 256))
    return out.reshape(B, S, D)
```
